```python
import jax, jax.numpy as jnp
from jax import lax
import numpy as np

D_MODEL = 1024
BATCH = 4
SEQ = 8192
DEPTH = 4

GRID_W = 64
CTX_LEN = 256
LRU_WIDTH = 512
LRU_HEADS = 8
LRU_BLOCK = LRU_WIDTH // LRU_HEADS
LRU_CONV = 4
LRU_C = 8.0
MLA_HEADS = 8
MLA_Q_RANK = 384
MLA_KV_RANK = 256
MLA_NOPE = 64
MLA_ROPE = 32
MLA_QK = MLA_NOPE + MLA_ROPE
MLA_V = 64
CONF_WIDTH = 512
CONF_K = 31
FFN_HIDDEN = 4 * D_MODEL
N_BRANCH = 3
Q_BLOCK = 128
ROPE_THETA = 10000.0
EPS = 1e-6
IN_SPLITS = (LRU_WIDTH, LRU_WIDTH, MLA_Q_RANK, MLA_KV_RANK, MLA_ROPE, 2 * CONF_WIDTH, N_BRANCH * D_MODEL)
D_IN = 512 + 512 + 384 + 256 + 32 + 1024 + 3072

kernel_name = "hybrid_lru_mla_conformer_dit"


def rms_norm(x, gain=None):
    xf = x.astype(jnp.float32)
    y = (xf * lax.rsqrt(jnp.mean(xf * xf, axis=-1, keepdims=True) + EPS)).astype(x.dtype)
    return y if gain is None else y * gain


def layer_norm(x, g, b):
    xf = x.astype(jnp.float32)
    mu = jnp.mean(xf, axis=-1, keepdims=True)
    var = jnp.mean(jnp.square(xf - mu), axis=-1, keepdims=True)
    return ((xf - mu) * lax.rsqrt(var + EPS)).astype(x.dtype) * g + b


def modulate(h, shift, scale):
    return h * (1 + scale) + shift


def split_in(p):
    outs, o = [], 0
    for w in IN_SPLITS:
        outs.append(p[..., o:o + w])
        o += w
    return outs


def dwconv(u, w, b, pad_l, pad_r):
    out = lax.conv_general_dilated(u, w[:, None, :], window_strides=(1,), padding=[(pad_l, pad_r)],
                                   dimension_numbers=('NWC', 'WIO', 'NWC'), feature_group_count=u.shape[-1])
    return out + b


def axial_rope_tables(seq):
    rows = seq // GRID_W
    row = jnp.repeat(jnp.arange(rows, dtype=jnp.float32), GRID_W)
    col = jnp.tile(jnp.arange(GRID_W, dtype=jnp.float32), rows)
    half = MLA_ROPE // 2
    freqs = ROPE_THETA ** (-jnp.arange(0, half, 2, dtype=jnp.float32) / half)
    ang = jnp.concatenate([row[:, None] * freqs, col[:, None] * freqs], axis=-1)
    return jnp.cos(ang), jnp.sin(ang)


def apply_rope(x, cos, sin):
    c = cos[None, :, None, :].astype(x.dtype)
    s = sin[None, :, None, :].astype(x.dtype)
    x1, x2 = x[..., :MLA_ROPE // 2], x[..., MLA_ROPE // 2:]
    return jnp.concatenate([x1 * c - x2 * s, x1 * s + x2 * c], axis=-1)


def rglru_coeffs(u, wa, ba, wx, bx, lam):
    uh = u.reshape(u.shape[:-1] + (LRU_HEADS, LRU_BLOCK))
    r = jax.nn.sigmoid(jnp.einsum('bthi,hij->bthj', uh, wa).reshape(u.shape) + ba)
    i = jax.nn.sigmoid(jnp.einsum('bthi,hij->bthj', uh, wx).reshape(u.shape) + bx)
    log_a = -LRU_C * r * jax.nn.softplus(-lam)
    a = jnp.exp(log_a)
    return a, jnp.sqrt(-jnp.expm1(2 * log_a)) * (i * u)


def linear_scan(a, b, reverse):
    def comb(l, r):
        return (l[0] * r[0], r[0] * l[1] + r[1])
    return lax.associative_scan(comb, (a, b), reverse=reverse, axis=1)


def rglru_bidir(u_x, P, h0_f, h0_b):
    u = dwconv(u_x, P['lru_conv_w'], P['lru_conv_b'], LRU_CONV // 2, LRU_CONV - 1 - LRU_CONV // 2)
    hs = []
    for d, (h0, rev) in enumerate(((h0_f, False), (h0_b, True))):
        a, b = rglru_coeffs(u, P['lru_wa'][d], P['lru_ba'][d], P['lru_wx'][d], P['lru_bx'][d], P['lru_lambda'][d])
        A, H = linear_scan(a, b, rev)
        hs.append(H + A * h0[:, None, :])
    return hs[0], hs[1]


def mla_qkv(cq, ckv, krope, P, cos, sin):
    lead = cq.shape[:-1]
    q = (rms_norm(cq, P['mla_q_norm']) @ P['mla_w_uq']).reshape(lead + (MLA_HEADS, MLA_QK))
    kv = (rms_norm(ckv, P['mla_kv_norm']) @ P['mla_w_ukv']).reshape(lead + (MLA_HEADS, MLA_NOPE + MLA_V))
    k_nope, v = kv[..., :MLA_NOPE], kv[..., MLA_NOPE:]
    k_r = jnp.broadcast_to(krope[:, :, None, :], lead + (MLA_HEADS, MLA_ROPE))
    k = jnp.concatenate([k_nope, k_r], axis=-1)
    q = rms_norm(q, P['mla_q_gain'])
    k = rms_norm(k, P['mla_k_gain'])
    if cos is not None:
        q = jnp.concatenate([q[..., :MLA_NOPE], apply_rope(q[..., MLA_NOPE:], cos, sin)], axis=-1)
        k = jnp.concatenate([k[..., :MLA_NOPE], apply_rope(k[..., MLA_NOPE:], cos, sin)], axis=-1)
    return q, k, v


def attend_blocks(q, k, v):
    bsz, t = q.shape[0], q.shape[1]
    nb = t // Q_BLOCK
    qb = q.reshape(bsz, nb, Q_BLOCK, MLA_HEADS, MLA_QK).transpose(1, 0, 2, 3, 4)
    scale = MLA_QK ** -0.5

    def one(qblk):
        s = jnp.einsum('bqhd,bkhd->bhqk', qblk, k).astype(jnp.float32) * scale
        p = jax.nn.softmax(s, axis=-1).astype(v.dtype)
        return jnp.einsum('bhqk,bkhd->bqhd', p, v)

    o = lax.map(one, qb)
    return o.transpose(1, 0, 2, 3, 4).reshape(bsz, t, MLA_HEADS * MLA_V)


def conformer_branch(u, P):
    a, g = jnp.split(u, 2, axis=-1)
    h = a * jax.nn.sigmoid(g)
    h = dwconv(h, P['conf_dw_w'], P['conf_dw_b'], CONF_K // 2, CONF_K // 2)
    h = jax.nn.silu(layer_norm(h, P['conf_ln_g'], P['conf_ln_b']))
    return h @ P['w_conf_o']


def merge(hf, hb, lgate, attn, conv_in, gate_logits, P):
    y_a = ((hf + hb) * jax.nn.gelu(lgate)) @ P['w_lru_o']
    y_b = attn @ P['w_mla_o']
    y_c = conformer_branch(conv_in, P)
    ga, gb, gc = jnp.split(jax.nn.sigmoid(gate_logits), N_BRANCH, axis=-1)
    return (ga * y_a + gb * y_b + gc * y_c) @ P['w_out']


def token_mixer(hl, hc, P, cos, sin, ctx_out):
    lx, lg, lq, lkv, lkr, lconv, lgate = split_in(hl @ P['w_in'])
    cx, cg, cq, ckv, ckr, cconv, cgate = split_in(hc @ P['w_in'])
    zeros = jnp.zeros((hc.shape[0], LRU_WIDTH), hc.dtype)
    chf, chb = rglru_bidir(cx, P, zeros, zeros)
    lhf, lhb = rglru_bidir(lx, P, chf[:, -1], chb[:, 0])
    qc, kc, vc = mla_qkv(cq, ckv, ckr, P, None, None)
    ql, kl, vl = mla_qkv(lq, lkv, lkr, P, cos, sin)
    attn_l = attend_blocks(ql, jnp.concatenate([kl, kc], axis=1), jnp.concatenate([vl, vc], axis=1))
    out_l = merge(lhf, lhb, lg, attn_l, lconv, lgate, P)
    if not ctx_out:
        return out_l, None
    attn_c = attend_blocks(qc, kc, vc)
    out_c = merge(chf, chb, cg, attn_c, cconv, cgate, P)
    return out_l, out_c


def ffn(h, P):
    return jnp.square(jax.nn.relu(h @ P['w_ff1'])) @ P['w_ff2']


def setup_inputs(seed: int = 0) -> dict:
    key = jax.random.key(seed)
    ks = iter(jax.random.split(key, 40))

    def nrm(shape, scale):
        return jax.random.normal(next(ks), shape, jnp.float32) * scale

    def gain(shape):
        return 1.0 + nrm(shape, 0.05)

    a0 = jax.random.uniform(next(ks), (DEPTH, 2, LRU_WIDTH), jnp.float32, 0.9, 0.999)
    return {
        'x': nrm((BATCH, SEQ, D_MODEL), 1.0),
        'c': nrm((BATCH, D_MODEL), 1.0),
        'ctx': nrm((BATCH, CTX_LEN, D_MODEL), 1.0),
        'c_ctx': nrm((D_MODEL,), 1.0),
        'w_ada': nrm((DEPTH, D_MODEL, 6 * D_MODEL), 0.5 * D_MODEL ** -0.5),
        'b_ada': nrm((DEPTH, 6 * D_MODEL), 0.02),
        'w_in': nrm((DEPTH, D_MODEL, D_IN), D_MODEL ** -0.5),
        'lru_conv_w': nrm((DEPTH, LRU_CONV, LRU_WIDTH), LRU_CONV ** -0.5),
        'lru_conv_b': nrm((DEPTH, LRU_WIDTH), 0.02),
        'lru_wa': nrm((DEPTH, 2, LRU_HEADS, LRU_BLOCK, LRU_BLOCK), LRU_BLOCK ** -0.5),
        'lru_ba': nrm((DEPTH, 2, LRU_WIDTH), 0.1),
        'lru_wx': nrm((DEPTH, 2, LRU_HEADS, LRU_BLOCK, LRU_BLOCK), LRU_BLOCK ** -0.5),
        'lru_bx': nrm((DEPTH, 2, LRU_WIDTH), 0.1),
        'lru_lambda': jnp.log(a0) - jnp.log1p(-a0),
        'w_lru_o': nrm((DEPTH, LRU_WIDTH, D_MODEL), LRU_WIDTH ** -0.5),
        'mla_q_norm': gain((DEPTH, MLA_Q_RANK)),
        'mla_w_uq': nrm((DEPTH, MLA_Q_RANK, MLA_HEADS * MLA_QK), MLA_Q_RANK ** -0.5),
        'mla_kv_norm': gain((DEPTH, MLA_KV_RANK)),
        'mla_w_ukv': nrm((DEPTH, MLA_KV_RANK, MLA_HEADS * (MLA_NOPE + MLA_V)), MLA_KV_RANK ** -0.5),
        'mla_q_gain': gain((DEPTH, MLA_QK)),
        'mla_k_gain': gain((DEPTH, MLA_QK)),
        'w_mla_o': nrm((DEPTH, MLA_HEADS * MLA_V, D_MODEL), (MLA_HEADS * MLA_V) ** -0.5),
        'conf_dw_w': nrm((DEPTH, CONF_K, CONF_WIDTH), CONF_K ** -0.5),
        'conf_dw_b': nrm((DEPTH, CONF_WIDTH), 0.02),
        'conf_ln_g': gain((DEPTH, CONF_WIDTH)),
        'conf_ln_b': nrm((DEPTH, CONF_WIDTH), 0.02),
        'w_conf_o': nrm((DEPTH, CONF_WIDTH, D_MODEL), CONF_WIDTH ** -0.5),
        'w_out': nrm((DEPTH, D_MODEL, D_MODEL), D_MODEL ** -0.5),
        'w_ff1': nrm((DEPTH, D_MODEL, FFN_HIDDEN), D_MODEL ** -0.5),
        'w_ff2': nrm((DEPTH, FFN_HIDDEN, D_MODEL), FFN_HIDDEN ** -0.5),
    }


def reference(x, c, ctx, c_ctx, w_ada, b_ada, w_in, lru_conv_w, lru_conv_b, lru_wa, lru_ba, lru_wx, lru_bx,
              lru_lambda, w_lru_o, mla_q_norm, mla_w_uq, mla_kv_norm, mla_w_ukv, mla_q_gain, mla_k_gain, w_mla_o,
              conf_dw_w, conf_dw_b, conf_ln_g, conf_ln_b, w_conf_o, w_out, w_ff1, w_ff2):
    cos, sin = axial_rope_tables(x.shape[1])
    s_lat = jax.nn.silu(c)
    s_ctx = jax.nn.silu(c_ctx)
    xl, xc = x, ctx
    for l in range(DEPTH):
        P = dict(w_in=w_in[l], lru_conv_w=lru_conv_w[l], lru_conv_b=lru_conv_b[l], lru_wa=lru_wa[l],
                 lru_ba=lru_ba[l], lru_wx=lru_wx[l], lru_bx=lru_bx[l], lru_lambda=lru_lambda[l],
                 w_lru_o=w_lru_o[l], mla_q_norm=mla_q_norm[l], mla_w_uq=mla_w_uq[l], mla_kv_norm=mla_kv_norm[l],
                 mla_w_ukv=mla_w_ukv[l], mla_q_gain=mla_q_gain[l], mla_k_gain=mla_k_gain[l], w_mla_o=w_mla_o[l],
                 conf_dw_w=conf_dw_w[l], conf_dw_b=conf_dw_b[l], conf_ln_g=conf_ln_g[l], conf_ln_b=conf_ln_b[l],
                 w_conf_o=w_conf_o[l], w_out=w_out[l], w_ff1=w_ff1[l], w_ff2=w_ff2[l])
        ctx_out = l < DEPTH - 1
        sh1, sc1, g1, sh2, sc2, g2 = jnp.split((s_lat @ w_ada[l] + b_ada[l])[:, None, :], 6, axis=-1)
        csh1, csc1, cg1, csh2, csc2, cg2 = jnp.split(s_ctx @ w_ada[l] + b_ada[l], 6, axis=-1)
        hl = modulate(rms_norm(xl), sh1, sc1)
        hc = modulate(rms_norm(xc), csh1, csc1)
        ml, mc = token_mixer(hl, hc, P, cos, sin, ctx_out)
        xl = xl + g1 * ml
        xl = xl + g2 * ffn(modulate(rms_norm(xl), sh2, sc2), P)
        if ctx_out:
            xc = xc + cg1 * mc
            xc = xc + cg2 * ffn(modulate(rms_norm(xc), csh2, csc2), P)
    return xl
```

```python
import functools
import math

import jax
import jax.numpy as jnp
import numpy as np
from jax import lax
from jax.experimental import pallas as pl
from jax.experimental.pallas import tpu as pltpu

F32 = jnp.float32
BF16 = jnp.bfloat16

GRID_W = 64
LRU_HEADS = 8
LRU_CONV = 4
LRU_C = 8.0
MLA_HEADS = 8
MLA_NOPE = 64
MLA_ROPE = 32
MLA_QK = MLA_NOPE + MLA_ROPE
MLA_V = 64
CONF_K = 31
ROPE_THETA = 10000.0
EPS = 1e-6

SUBLANES = 8
LANES = 128
HEAD_PAD = LANES
TILE = 256
SCAN_SEGS = SUBLANES
SEG_LEN = TILE // SCAN_SEGS
SEG_PITCH = SEG_LEN + 4
CONV_HALO = 16
LRU_HALO = 8
VMEM_LIMIT = 56 * 1024 * 1024
LOG2E = 1.4426950408889634
NEG_BIG = -1e30


def _cparams(sem):
    return pltpu.CompilerParams(dimension_semantics=sem, vmem_limit_bytes=VMEM_LIMIT)


def _const_spec(shape):
    nd = len(shape)
    return pl.BlockSpec(shape, lambda *_: (0,) * nd)


def _sigmoid(x):
    return 1.0 / (1.0 + jnp.exp(-x))


def _silu(x):
    return x * _sigmoid(x)


def _gelu_tanh(x):
    return 0.5 * x * (1.0 + jnp.tanh(math.sqrt(2.0 / math.pi) * (x + 0.044715 * (x * x * x))))


def _rms(x, width):
    return x * lax.rsqrt(jnp.sum(x * x, axis=-1, keepdims=True) * (1.0 / width) + EPS)


def _dot(a, b):
    return jnp.dot(a, b, preferred_element_type=F32)


def _ada_kernel(c_ref, w_ref, b_ref, o_ref):
    s = _silu(c_ref[...])
    o_ref[...] = _dot(s.astype(BF16), w_ref[...].astype(BF16)) + b_ref[...]


def _ada_call(cc, w_ada, b_ada):
    depth, d, d6 = w_ada.shape
    rows = cc.shape[0]
    nt = d
    return pl.pallas_call(
        _ada_kernel,
        grid=(depth, d6 // nt),
        in_specs=[
            pl.BlockSpec((rows, d), lambda l, j: (0, 0)),
            pl.BlockSpec((None, d, nt), lambda l, j: (l, 0, j)),
            pl.BlockSpec((None, 1, nt), lambda l, j: (l, 0, j)),
        ],
        out_specs=pl.BlockSpec((None, rows, nt), lambda l, j: (l, 0, j)),
        out_shape=jax.ShapeDtypeStruct((depth, rows, d6), F32),
        compiler_params=_cparams(("parallel", "parallel")),
        name="ada",
    )(cc, w_ada, b_ada.reshape(depth, 1, d6))


def _mod_row(mod_ref, ctx_row):
    b = pl.program_id(0)
    t = pl.program_id(1)
    return jnp.where(t == 0, ctx_row, b)


def _rope(xh, c, s1, s2):
    return xh * c + pltpu.roll(xh, 16, 1) * s1 + pltpu.roll(xh, HEAD_PAD - 16, 1) * s2


def _inproj_kernel(ctx_row, dims, x_ref, mod_ref, w1_ref, qn_ref, wuq_ref, kvn_ref, wukv_ref,
                   qg_ref, kg_ref, rope_ref, lx_ref, lg_ref, glu_ref, q_ref, k_ref, v_ref):
    d, lw, qr, kvr, cw = dims
    row = _mod_row(mod_ref, ctx_row)
    mod = mod_ref[pl.ds(row, 1), :]
    sh, sc = mod[:, 0:d], mod[:, d:2 * d]
    h = _rms(x_ref[...], d) * (1.0 + sc) + sh
    p = _dot(h.astype(BF16), w1_ref[...])
    o = 0
    lx_ref[...] = p[:, o:o + lw]; o += lw
    lg_ref[...] = p[:, o:o + lw]; o += lw
    cq = p[:, o:o + qr]; o += qr
    ckv = p[:, o:o + kvr]; o += kvr
    ca = p[:, o:o + cw]; o += cw
    cg = p[:, o:o + cw]; o += cw
    kr = p[:, o:o + HEAD_PAD]
    glu_ref[...] = ca * _sigmoid(cg)

    rc = rope_ref[:, 0:HEAD_PAD]
    rs1 = rope_ref[:, HEAD_PAD:2 * HEAD_PAD]
    rs2 = rope_ref[:, 2 * HEAD_PAD:3 * HEAD_PAD]

    cqn = _rms(cq, qr) * qn_ref[...]
    qraw = _dot(cqn.astype(BF16), wuq_ref[...])
    ckvn = _rms(ckv, kvr) * kvn_ref[...]
    kvraw = _dot(ckvn.astype(BF16), wukv_ref[...])
    qg = qg_ref[...]
    kg = kg_ref[...]
    for hd in range(MLA_HEADS):
        sl = slice(hd * HEAD_PAD, (hd + 1) * HEAD_PAD)
        qh = _rms(qraw[:, sl], MLA_QK) * qg
        q_ref[:, sl] = _rope(qh, rc, rs1, rs2).astype(BF16)
        kh = _rms(kvraw[:, sl] + kr, MLA_QK) * kg
        k_ref[:, sl] = _rope(kh, rc, rs1, rs2).astype(BF16)
    v_ref[...] = kvraw[:, MLA_HEADS * HEAD_PAD:].astype(BF16)


def _inproj_call(x_all, mod, w1, qn, wuq, kvn, wukv, qg, kg, rope, ctx_row):
    b, n, d = x_all.shape
    lw = 512
    qr, kvr, cw = wuq.shape[0], wukv.shape[0], 512
    hq = MLA_HEADS * HEAD_PAD
    hv = MLA_HEADS * MLA_V
    nt = n // TILE
    tok = lambda w: pl.BlockSpec((None, TILE, w), lambda bb, t: (bb, t, 0))
    kern = functools.partial(_inproj_kernel, ctx_row, (d, lw, qr, kvr, cw))
    return pl.pallas_call(
        kern,
        grid=(b, nt),
        in_specs=[
            tok(d),
            _const_spec(mod.shape),
            _const_spec(w1.shape),
            _const_spec(qn.shape),
            _const_spec(wuq.shape),
            _const_spec(kvn.shape),
            _const_spec(wukv.shape),
            _const_spec(qg.shape),
            _const_spec(kg.shape),
            pl.BlockSpec((TILE, 3 * HEAD_PAD), lambda bb, t: (t, 0)),
        ],
        out_specs=[tok(lw), tok(lw), tok(cw), tok(hq), tok(hq), tok(hv)],
        out_shape=[
            jax.ShapeDtypeStruct((b, n, lw), F32),
            jax.ShapeDtypeStruct((b, n, lw), F32),
            jax.ShapeDtypeStruct((b, n, cw), F32),
            jax.ShapeDtypeStruct((b, n, hq), BF16),
            jax.ShapeDtypeStruct((b, n, hq), BF16),
            jax.ShapeDtypeStruct((b, n, hv), BF16),
        ],
        compiler_params=_cparams(("parallel", "parallel")),
        name="inproj",
    )(x_all, mod, w1, qn, wuq, kvn, wukv, qg, kg, rope)


def _lru_tile_index(t, nt, reverse):
    if reverse:
        return jnp.where(t == 0, 0, nt - t)
    return t


def _lru_kernel(reverse, nt, lw, x_ref, xp_ref, xn_ref, par_ref, wg_ref, *rest):
    if reverse:
        o_ref, xe_s, a_s, b_s, p_s, h_s, st_s = rest
    else:
        hb_ref, lg_ref, o_ref, xe_s, a_s, b_s, p_s, h_s, st_s = rest
    t = pl.program_id(1)
    tile = _lru_tile_index(t, nt, reverse)
    nchunk = lw // LANES

    prev_ok = tile >= 2
    next_ok = jnp.logical_and(tile >= 1, tile <= nt - 2)
    xe_s[0:LRU_HALO, :] = jnp.where(prev_ok, xp_ref[...], 0.0)
    xe_s[LRU_HALO:LRU_HALO + TILE, :] = x_ref[...]
    xe_s[LRU_HALO + TILE:, :] = jnp.where(next_ok, xn_ref[...], 0.0)
    ba = par_ref[0:1, :]
    bx = par_ref[1:2, :]
    lam = par_ref[2:3, :]
    u = jnp.zeros((TILE, lw), F32) + par_ref[3:4, :]
    for k in range(LRU_CONV):
        off = LRU_HALO - LRU_CONV // 2 + k
        u = u + xe_s[off:off + TILE, :] * par_ref[4 + k:5 + k, :]

    sp = jnp.maximum(-lam, 0.0) + jnp.log1p(jnp.exp(-jnp.abs(lam)))
    decay = (-LRU_C) * sp

    @pl.when(t == 0)
    def _():
        st_s[...] = jnp.zeros_like(st_s)

    for c in range(nchunk):
        cs = slice(c * LANES, (c + 1) * LANES)
        uc = u[:, cs]
        g = _dot(uc.astype(BF16), wg_ref[c])
        r = _sigmoid(g[:, 0:LANES] + ba[:, cs])
        i = _sigmoid(g[:, LANES:] + bx[:, cs])
        a = jnp.exp(r * decay[:, cs])
        bb = jnp.sqrt(1.0 - a * a) * (i * uc)
        for s in range(SCAN_SEGS):
            rows = slice(s * SEG_LEN, (s + 1) * SEG_LEN)
            a_s[c, s * SEG_PITCH:s * SEG_PITCH + SEG_LEN, :] = a[rows]
            b_s[c, s * SEG_PITCH:s * SEG_PITCH + SEG_LEN, :] = bb[rows]

    def step(jj, carry):
        j = (SEG_LEN - 1 - jj) if reverse else jj
        new = []
        for c in range(nchunk):
            hh, pp = carry[2 * c], carry[2 * c + 1]
            av = a_s[c, pl.ds(j, SCAN_SEGS, stride=SEG_PITCH), :]
            bv = b_s[c, pl.ds(j, SCAN_SEGS, stride=SEG_PITCH), :]
            hh = av * hh + bv
            pp = av * pp
            h_s[c, pl.ds(j, SCAN_SEGS, stride=SEG_PITCH), :] = hh
            p_s[c, pl.ds(j, SCAN_SEGS, stride=SEG_PITCH), :] = pp
            new += [hh, pp]
        return tuple(new)

    init = []
    for c in range(nchunk):
        init += [jnp.zeros((SCAN_SEGS, LANES), F32), jnp.ones((SCAN_SEGS, LANES), F32)]
    fin = lax.fori_loop(0, SEG_LEN, step, tuple(init))

    order = range(SCAN_SEGS - 1, -1, -1) if reverse else range(SCAN_SEGS)
    for c in range(nchunk):
        cs = slice(c * LANES, (c + 1) * LANES)
        hfin, pfin = fin[2 * c], fin[2 * c + 1]
        carry = st_s[:, cs]
        for s in order:
            base = s * SEG_PITCH
            hseg = h_s[c, base:base + SEG_LEN, :] + p_s[c, base:base + SEG_LEN, :] * carry
            rows = slice(s * SEG_LEN, (s + 1) * SEG_LEN)
            if reverse:
                o_ref[rows, cs] = hseg
            else:
                tot = hseg + hb_ref[rows, cs]
                o_ref[rows, cs] = (tot * _gelu_tanh(lg_ref[rows, cs])).astype(o_ref.dtype)
            carry = hfin[s:s + 1, :] + pfin[s:s + 1, :] * carry
        st_s[:, cs] = carry


def _lru_call(reverse, lx, par, wg, hb=None, lg=None):
    b, n, lw = lx.shape
    nt = n // TILE
    nrow8 = n // LRU_HALO
    per = TILE // LRU_HALO
    tidx = lambda t: _lru_tile_index(t, nt, reverse)
    tok = pl.BlockSpec((None, TILE, lw), lambda bb, t: (bb, tidx(t), 0))
    prev = pl.BlockSpec((None, LRU_HALO, lw), lambda bb, t: (bb, jnp.maximum(tidx(t) * per - 1, 0), 0))
    nxt = pl.BlockSpec((None, LRU_HALO, lw),
                       lambda bb, t: (bb, jnp.minimum((tidx(t) + 1) * per, nrow8 - 1), 0))
    nchunk = lw // LANES
    seg_rows = SCAN_SEGS * SEG_PITCH
    scratch = [
        pltpu.VMEM((TILE + 2 * LRU_HALO, lw), F32),
        pltpu.VMEM((nchunk, seg_rows, LANES), F32),
        pltpu.VMEM((nchunk, seg_rows, LANES), F32),
        pltpu.VMEM((nchunk, seg_rows, LANES), F32),
        pltpu.VMEM((nchunk, seg_rows, LANES), F32),
        pltpu.VMEM((1, lw), F32),
    ]
    in_specs = [tok, prev, nxt, _const_spec(par.shape), _const_spec(wg.shape)]
    args = [lx, lx, lx, par, wg]
    if reverse:
        out_dtype = F32
    else:
        in_specs += [tok, tok]
        args += [hb, lg]
        out_dtype = BF16
    return pl.pallas_call(
        functools.partial(_lru_kernel, reverse, nt, lw),
        grid=(b, nt),
        in_specs=in_specs,
        out_specs=tok,
        out_shape=jax.ShapeDtypeStruct((b, n, lw), out_dtype),
        scratch_shapes=scratch,
        compiler_params=_cparams(("parallel", "arbitrary")),
        name="lru_bwd" if reverse else "lru_fwd",
    )(*args)


def _conf_kernel(nt, cw, x_ref, xp_ref, xn_ref, w_ref, par_ref, o_ref, xe_s):
    t = pl.program_id(1)
    prev_ok = t >= 2
    next_ok = jnp.logical_and(t >= 1, t <= nt - 2)
    xe_s[0:CONV_HALO, :] = jnp.where(prev_ok, xp_ref[...], 0.0)
    xe_s[CONV_HALO:CONV_HALO + TILE, :] = x_ref[...]
    xe_s[CONV_HALO + TILE:, :] = jnp.where(next_ok, xn_ref[...], 0.0)
    acc = jnp.zeros((TILE, cw), F32) + par_ref[0:1, :]
    for k in range(CONF_K):
        off = CONV_HALO - CONF_K // 2 + k
        acc = acc + xe_s[off:off + TILE, :] * w_ref[k:k + 1, :]
    mu = jnp.mean(acc, axis=-1, keepdims=True)
    xc = acc - mu
    var = jnp.mean(xc * xc, axis=-1, keepdims=True)
    y = xc * lax.rsqrt(var + EPS) * par_ref[1:2, :] + par_ref[2:3, :]
    o_ref[...] = _silu(y).astype(o_ref.dtype)


def _conf_call(glu, w, par):
    b, n, cw = glu.shape
    nt = n // TILE
    nrow = n // CONV_HALO
    per = TILE // CONV_HALO
    tok = pl.BlockSpec((None, TILE, cw), lambda bb, t: (bb, t, 0))
    prev = pl.BlockSpec((None, CONV_HALO, cw), lambda bb, t: (bb, jnp.maximum(t * per - 1, 0), 0))
    nxt = pl.BlockSpec((None, CONV_HALO, cw), lambda bb, t: (bb, jnp.minimum((t + 1) * per, nrow - 1), 0))
    return pl.pallas_call(
        functools.partial(_conf_kernel, nt, cw),
        grid=(b, nt),
        in_specs=[tok, prev, nxt, _const_spec(w.shape), _const_spec(par.shape)],
        out_specs=tok,
        out_shape=jax.ShapeDtypeStruct((b, n, cw), BF16),
        scratch_shapes=[pltpu.VMEM((TILE + 2 * CONV_HALO, cw), F32)],
        compiler_params=_cparams(("parallel", "parallel")),
        name="conformer",
    )(glu, glu, glu, w, par)


def _attn_kernel(ctx_len, n, tk, q_ref, k_ref, v_ref, o_ref):
    qt = pl.program_id(2)
    n_lat_chunks = (n - ctx_len) // tk
    trips = jnp.where(qt == 0, 0, n_lat_chunks)
    outs = []
    for hd in range(2):
        hs = slice(hd * HEAD_PAD, (hd + 1) * HEAD_PAD)
        q = q_ref[:, hs]

        def scores(kc):
            return lax.dot_general(q, kc, (((1,), (1,)), ((), ())), preferred_element_type=F32)

        s = scores(k_ref[0:ctx_len, hs])
        m = jnp.max(s, axis=-1, keepdims=True)
        p = jnp.exp2(s - m)
        l = jnp.sum(p, axis=-1, keepdims=True)
        acc = _dot(p.astype(BF16), v_ref[0:ctx_len, :])

        def body(j, carry):
            m, l, acc = carry
            start = pl.multiple_of(ctx_len + j * tk, tk)
            s = scores(k_ref[pl.ds(start, tk), hs])
            m_new = jnp.maximum(m, jnp.max(s, axis=-1, keepdims=True))
            alpha = jnp.exp2(m - m_new)
            p = jnp.exp2(s - m_new)
            l = alpha * l + jnp.sum(p, axis=-1, keepdims=True)
            acc = alpha * acc + _dot(p.astype(BF16), v_ref[pl.ds(start, tk), :])
            return m_new, l, acc

        m, l, acc = lax.fori_loop(0, trips, body, (m, l, acc))
        outs.append(acc / l)
    lane = lax.broadcasted_iota(jnp.int32, outs[0].shape, 1)
    o_ref[...] = jnp.where(lane < MLA_V, outs[0], outs[1]).astype(o_ref.dtype)


def _attn_call(q, k, v, ctx_len):
    b, n, hq = q.shape
    npair = MLA_HEADS // 2
    nq = n // TILE
    tk = 512
    return pl.pallas_call(
        functools.partial(_attn_kernel, ctx_len, n, tk),
        grid=(b, npair, nq),
        in_specs=[
            pl.BlockSpec((None, TILE, 2 * HEAD_PAD), lambda bb, hp, t: (bb, t, hp)),
            pl.BlockSpec((None, n, 2 * HEAD_PAD), lambda bb, hp, t: (bb, 0, hp)),
            pl.BlockSpec((None, n, 2 * MLA_V), lambda bb, hp, t: (bb, 0, hp)),
        ],
        out_specs=pl.BlockSpec((None, TILE, 2 * MLA_V), lambda bb, hp, t: (bb, t, hp)),
        out_shape=jax.ShapeDtypeStruct((b, n, MLA_HEADS * MLA_V), BF16),
        compiler_params=_cparams(("parallel", "parallel", "arbitrary")),
        name="attention",
    )(q, k, v)


def _merge_kernel(ctx_row, t0, d, x_ref, mod_ref, ya_ref, yb_ref, yc_ref, wgate_ref, wa_ref, wb_ref, wc_ref,
                  wo_ref, o_ref):
    b = pl.program_id(0)
    t = pl.program_id(1) + t0
    row = jnp.where(t == 0, ctx_row, b)
    mod = mod_ref[pl.ds(row, 1), :]
    sh, sc, g1 = mod[:, 0:d], mod[:, d:2 * d], mod[:, 2 * d:3 * d]
    x = x_ref[...]
    h = _rms(x, d) * (1.0 + sc) + sh
    gates = _sigmoid(_dot(h.astype(BF16), wgate_ref[...]))
    y_a = _dot(ya_ref[...], wa_ref[...])
    y_b = _dot(yb_ref[...], wb_ref[...])
    y_c = _dot(yc_ref[...], wc_ref[...])
    mix = gates[:, 0:d] * y_a + gates[:, d:2 * d] * y_b + gates[:, 2 * d:3 * d] * y_c
    o_ref[...] = x + g1 * _dot(mix.astype(BF16), wo_ref[...])


def _merge_call(x_all, mod, ya, yb, yc, wgate, wa, wb, wc, wo, ctx_row, t0):
    b, n, d = x_all.shape
    nt = n // TILE - t0
    tok = lambda w: pl.BlockSpec((None, TILE, w), lambda bb, t: (bb, t + t0, 0))
    tok_out = pl.BlockSpec((None, TILE, d), lambda bb, t: (bb, t, 0))
    return pl.pallas_call(
        functools.partial(_merge_kernel, ctx_row, t0, d),
        grid=(b, nt),
        in_specs=[tok(d), _const_spec(mod.shape), tok(ya.shape[-1]), tok(yb.shape[-1]), tok(yc.shape[-1]),
                  _const_spec(wgate.shape), _const_spec(wa.shape), _const_spec(wb.shape),
                  _const_spec(wc.shape), _const_spec(wo.shape)],
        out_specs=tok_out,
        out_shape=jax.ShapeDtypeStruct((b, n - t0 * TILE, d), F32),
        compiler_params=_cparams(("parallel", "parallel")),
        name="merge",
    )(x_all, mod, ya, yb, yc, wgate, wa, wb, wc, wo)


def _ffn_kernel(ctx_row, t0, d, x_ref, mod_ref, w1_ref, w2_ref, o_ref):
    b = pl.program_id(0)
    t = pl.program_id(1) + t0
    row = jnp.where(t == 0, ctx_row, b)
    mod = mod_ref[pl.ds(row, 1), :]
    sh, sc, g2 = mod[:, 3 * d:4 * d], mod[:, 4 * d:5 * d], mod[:, 5 * d:6 * d]
    x = x_ref[...]
    h = _rms(x, d) * (1.0 + sc) + sh
    a = jnp.maximum(_dot(h.astype(BF16), w1_ref[...]), 0.0)
    o_ref[...] = x + g2 * _dot((a * a).astype(BF16), w2_ref[...])


def _ffn_call(x_all, mod, w1, w2, ctx_row, t0):
    b, n, d = x_all.shape
    nt = n // TILE
    tok = pl.BlockSpec((None, TILE, d), lambda bb, t: (bb, t, 0))
    return pl.pallas_call(
        functools.partial(_ffn_kernel, ctx_row, t0, d),
        grid=(b, nt),
        in_specs=[tok, _const_spec(mod.shape), _const_spec(w1.shape), _const_spec(w2.shape)],
        out_specs=tok,
        out_shape=jax.ShapeDtypeStruct((b, n, d), F32),
        compiler_params=_cparams(("parallel", "parallel")),
        name="ffn",
    )(x_all, mod, w1, w2)


def _pad_heads(w, width):
    lead = w.shape[:-1]
    w = w.reshape(lead + (MLA_HEADS, width))
    w = jnp.pad(w, [(0, 0)] * len(lead) + [(0, 0), (0, HEAD_PAD - width)])
    return w.reshape(lead + (MLA_HEADS * HEAD_PAD,))


def _block_diag_pairs(w):
    hh, bi, bj = w.shape
    w4 = w.reshape(hh // 2, 2, bi, bj)
    eye = jnp.eye(2, dtype=w.dtype)
    return jnp.einsum('cpij,pq->cpiqj', w4, eye).reshape(hh // 2, 2 * bi, 2 * bj)


def _rope_tables(ctx_len, seq):
    rows = seq // GRID_W
    row = np.repeat(np.arange(rows, dtype=np.float32), GRID_W)
    col = np.tile(np.arange(GRID_W, dtype=np.float32), rows)
    half = MLA_ROPE // 2
    freqs = jnp.asarray(ROPE_THETA, F32) ** (-jnp.arange(0, half, 2, dtype=F32) / half)
    ang = jnp.concatenate([row[:, None] * freqs, col[:, None] * freqs], axis=-1)
    cos, sin = jnp.cos(ang), jnp.sin(ang)
    n = ctx_len + seq
    c = jnp.ones((n, HEAD_PAD), F32)
    c = c.at[ctx_len:, MLA_NOPE:MLA_NOPE + half].set(cos).at[ctx_len:, MLA_NOPE + half:MLA_QK].set(cos)
    s1 = jnp.zeros((n, HEAD_PAD), F32).at[ctx_len:, MLA_NOPE + half:MLA_QK].set(sin)
    s2 = jnp.zeros((n, HEAD_PAD), F32).at[ctx_len:, MLA_NOPE:MLA_NOPE + half].set(-sin)
    return jnp.concatenate([c, s1, s2], axis=-1)


def kernel(x, c, ctx, c_ctx, w_ada, b_ada, w_in, lru_conv_w, lru_conv_b, lru_wa, lru_ba, lru_wx, lru_bx,
           lru_lambda, w_lru_o, mla_q_norm, mla_w_uq, mla_kv_norm, mla_w_ukv, mla_q_gain, mla_k_gain, w_mla_o,
           conf_dw_w, conf_dw_b, conf_ln_g, conf_ln_b, w_conf_o, w_out, w_ff1, w_ff2):
    bsz, seq, d = x.shape
    ctx_len = ctx.shape[1]
    depth = w_ada.shape[0]
    lw = lru_conv_w.shape[-1]
    qr = mla_w_uq.shape[1]
    kvr = mla_w_ukv.shape[1]
    cw = conf_dw_w.shape[-1]
    assert ctx_len == TILE and seq % TILE == 0 and bsz < SUBLANES
    assert lw % LANES == 0 and d % LANES == 0

    ctx_row = bsz
    cc = jnp.zeros((SUBLANES, d), F32).at[:bsz].set(c).at[ctx_row].set(c_ctx)
    mods = _ada_call(cc, w_ada, b_ada)
    rope = _rope_tables(ctx_len, seq)
    x_all = jnp.concatenate([ctx, x], axis=1)
    q_scale = (MLA_QK ** -0.5) * LOG2E

    o_lg, o_cq, o_ckv, o_kr, o_conv, o_gate = (lw, 2 * lw, 2 * lw + qr, 2 * lw + qr + kvr,
                                               2 * lw + qr + kvr + MLA_ROPE,
                                               2 * lw + qr + kvr + MLA_ROPE + 2 * cw)
    for l in range(depth):
        wi = w_in[l]
        kr_pad = jnp.zeros((d, HEAD_PAD), F32).at[:, MLA_NOPE:MLA_QK].set(wi[:, o_kr:o_conv])
        w1 = jnp.concatenate([wi[:, :o_kr], wi[:, o_conv:o_gate], kr_pad], axis=1).astype(BF16)
        wgate = wi[:, o_gate:].astype(BF16)
        wuq = _pad_heads(mla_w_uq[l], MLA_QK).astype(BF16)
        ukv = mla_w_ukv[l].reshape(kvr, MLA_HEADS, MLA_NOPE + MLA_V)
        wk = _pad_heads(ukv[:, :, :MLA_NOPE].reshape(kvr, MLA_HEADS * MLA_NOPE), MLA_NOPE)
        wv = ukv[:, :, MLA_NOPE:].reshape(kvr, MLA_HEADS * MLA_V)
        wukv = jnp.concatenate([wk, wv], axis=1).astype(BF16)
        qg = (jnp.pad(mla_q_gain[l], (0, HEAD_PAD - MLA_QK)) * q_scale).reshape(1, HEAD_PAD)
        kg = jnp.pad(mla_k_gain[l], (0, HEAD_PAD - MLA_QK)).reshape(1, HEAD_PAD)
        mod = mods[l]

        lx, lg, glu, q, k, v = _inproj_call(
            x_all, mod, w1, mla_q_norm[l].reshape(1, qr), wuq, mla_kv_norm[l].reshape(1, kvr), wukv,
            qg, kg, rope, ctx_row)

        def lru_par(dd):
            return jnp.concatenate([lru_ba[l, dd][None], lru_bx[l, dd][None], lru_lambda[l, dd][None],
                                    lru_conv_b[l][None], lru_conv_w[l]], axis=0)

        def lru_w(dd):
            return jnp.concatenate([_block_diag_pairs(lru_wa[l, dd]), _block_diag_pairs(lru_wx[l, dd])],
                                   axis=-1).astype(BF16)

        hb = _lru_call(True, lx, lru_par(1), lru_w(1))
        ya = _lru_call(False, lx, lru_par(0), lru_w(0), hb, lg)

        conf_par = jnp.zeros((SUBLANES, cw), F32).at[0].set(conf_dw_b[l]).at[1].set(conf_ln_g[l]).at[2].set(
            conf_ln_b[l])
        conf_w = jnp.zeros((4 * SUBLANES, cw), F32).at[:CONF_K].set(conf_dw_w[l])
        yc = _conf_call(glu, conf_w, conf_par)

        yb = _attn_call(q, k, v, ctx_len)

        t0 = 0 if l < depth - 1 else 1
        x_mid = _merge_call(x_all, mod, ya, yb, yc, wgate, w_lru_o[l].astype(BF16), w_mla_o[l].astype(BF16),
                            w_conf_o[l].astype(BF16), w_out[l].astype(BF16), ctx_row, t0)
        x_all = _ffn_call(x_mid, mod, w_ff1[l].astype(BF16), w_ff2[l].astype(BF16), ctx_row, t0)
    return x_all
```

```python
import functools
import math

import jax
import jax.numpy as jnp
import numpy as np
from jax import lax
from jax.experimental import pallas as pl
from jax.experimental.pallas import tpu as pltpu

F32 = jnp.float32
BF16 = jnp.bfloat16

GRID_W = 64
LRU_HEADS = 8
LRU_CONV = 4
LRU_C = 8.0
MLA_HEADS = 8
MLA_NOPE = 64
MLA_ROPE = 32
MLA_QK = MLA_NOPE + MLA_ROPE
MLA_V = 64
CONF_K = 31
ROPE_THETA = 10000.0
EPS = 1e-6

SUBLANES = 8
LANES = 128
HEAD_PAD = LANES
TILE = 256
SCAN_SEGS = SUBLANES
SEG_LEN = TILE // SCAN_SEGS
SEG_PITCH = SEG_LEN + 4
CONV_HALO = 16
CONF_ROWS = 32
LRU_HALO = 8
ATTN_UNROLL = 16
VMEM_LIMIT = 56 * 1024 * 1024
LOG2E = 1.4426950408889634
NEG_BIG = -1e30


def _cparams(sem):
    return pltpu.CompilerParams(dimension_semantics=sem, vmem_limit_bytes=VMEM_LIMIT)


def _const_spec(shape):
    nd = len(shape)
    return pl.BlockSpec(shape, lambda *_: (0,) * nd)


def _sigmoid(x):
    return 1.0 / (1.0 + jnp.exp(-x))


def _silu(x):
    return x * _sigmoid(x)


def _gelu_tanh(x):
    return 0.5 * x * (1.0 + jnp.tanh(math.sqrt(2.0 / math.pi) * (x + 0.044715 * (x * x * x))))


def _rms(x, width):
    return x * lax.rsqrt(jnp.sum(x * x, axis=-1, keepdims=True) * (1.0 / width) + EPS)


def _dot(a, b):
    return jnp.dot(a, b, preferred_element_type=F32)


def _ada_kernel(c_ref, w_ref, b_ref, o_ref):
    s = _silu(c_ref[...])
    o_ref[...] = _dot(s.astype(BF16), w_ref[...].astype(BF16)) + b_ref[...]


def _ada_call(cc, w_ada, b_ada):
    depth, d, d6 = w_ada.shape
    rows = cc.shape[0]
    nt = d
    return pl.pallas_call(
        _ada_kernel,
        grid=(depth, d6 // nt),
        in_specs=[
            pl.BlockSpec((rows, d), lambda l, j: (0, 0)),
            pl.BlockSpec((None, d, nt), lambda l, j: (l, 0, j)),
            pl.BlockSpec((None, 1, nt), lambda l, j: (l, 0, j)),
        ],
        out_specs=pl.BlockSpec((None, rows, nt), lambda l, j: (l, 0, j)),
        out_shape=jax.ShapeDtypeStruct((depth, rows, d6), F32),
        compiler_params=_cparams(("parallel", "parallel")),
        name="ada",
    )(cc, w_ada, b_ada.reshape(depth, 1, d6))


def _mod_row(mod_ref, ctx_row):
    b = pl.program_id(0)
    t = pl.program_id(1)
    return jnp.where(t == 0, ctx_row, b)


def _rope(xh, c, s1, s2):
    return xh * c + pltpu.roll(xh, 16, 1) * s1 + pltpu.roll(xh, HEAD_PAD - 16, 1) * s2


def _inproj_kernel(ctx_row, dims, x_ref, mod_ref, w1_ref, qn_ref, wuq_ref, kvn_ref, wukv_ref,
                   qg_ref, kg_ref, rope_ref, lx_ref, lg_ref, glu_ref, q_ref, k_ref, v_ref):
    d, lw, qr, kvr, cw = dims
    row = _mod_row(mod_ref, ctx_row)
    mod = mod_ref[pl.ds(row, 1), :]
    sh, sc = mod[:, 0:d], mod[:, d:2 * d]
    h = (_rms(x_ref[...], d) * (1.0 + sc) + sh).astype(BF16)
    n_mla = qr + kvr + HEAD_PAD
    pm = _dot(h, w1_ref[:, 0:n_mla])
    cq = pm[:, 0:qr]
    ckv = pm[:, qr:qr + kvr]
    kr = pm[:, qr + kvr:n_mla]
    cqn = _rms(cq, qr) * qn_ref[...]
    qraw = _dot(cqn.astype(BF16), wuq_ref[...])
    ckvn = _rms(ckv, kvr) * kvn_ref[...]
    kvraw = _dot(ckvn.astype(BF16), wukv_ref[...])

    p = _dot(h, w1_ref[:, n_mla:])
    lx_ref[...] = p[:, 0:lw]
    lg_ref[...] = p[:, lw:2 * lw]
    ca = p[:, 2 * lw:2 * lw + cw]
    cg = p[:, 2 * lw + cw:2 * lw + 2 * cw]
    glu_ref[...] = ca * _sigmoid(cg)

    rc = rope_ref[:, 0:HEAD_PAD]
    rs1 = rope_ref[:, HEAD_PAD:2 * HEAD_PAD]
    rs2 = rope_ref[:, 2 * HEAD_PAD:3 * HEAD_PAD]
    qg = qg_ref[...]
    kg = kg_ref[...]
    one_col = jnp.where(lax.broadcasted_iota(jnp.int32, (1, HEAD_PAD), 1) == MLA_V, 1.0, 0.0)
    for hd in range(MLA_HEADS):
        sl = slice(hd * HEAD_PAD, (hd + 1) * HEAD_PAD)
        qh = _rms(qraw[:, sl], MLA_QK) * qg
        q_ref[:, sl] = _rope(qh, rc, rs1, rs2).astype(BF16)
        kh = _rms(kvraw[:, sl] + kr, MLA_QK) * kg
        k_ref[:, sl] = _rope(kh, rc, rs1, rs2).astype(BF16)
        vo = (MLA_HEADS + hd) * HEAD_PAD
        v_ref[:, sl] = (kvraw[:, vo:vo + HEAD_PAD] + one_col).astype(BF16)


def _inproj_call(x_all, mod, w1, qn, wuq, kvn, wukv, qg, kg, rope, ctx_row):
    b, n, d = x_all.shape
    lw = 512
    qr, kvr, cw = wuq.shape[0], wukv.shape[0], 512
    hq = MLA_HEADS * HEAD_PAD
    hv = hq
    nt = n // TILE
    tok = lambda w: pl.BlockSpec((None, TILE, w), lambda bb, t: (bb, t, 0))
    kern = functools.partial(_inproj_kernel, ctx_row, (d, lw, qr, kvr, cw))
    return pl.pallas_call(
        kern,
        grid=(b, nt),
        in_specs=[
            tok(d),
            _const_spec(mod.shape),
            _const_spec(w1.shape),
            _const_spec(qn.shape),
            _const_spec(wuq.shape),
            _const_spec(kvn.shape),
            _const_spec(wukv.shape),
            _const_spec(qg.shape),
            _const_spec(kg.shape),
            pl.BlockSpec((TILE, 3 * HEAD_PAD), lambda bb, t: (t, 0)),
        ],
        out_specs=[tok(lw), tok(lw), tok(cw), tok(hq), tok(hq), tok(hv)],
        out_shape=[
            jax.ShapeDtypeStruct((b, n, lw), F32),
            jax.ShapeDtypeStruct((b, n, lw), F32),
            jax.ShapeDtypeStruct((b, n, cw), F32),
            jax.ShapeDtypeStruct((b, n, hq), BF16),
            jax.ShapeDtypeStruct((b, n, hq), BF16),
            jax.ShapeDtypeStruct((b, n, hv), BF16),
        ],
        compiler_params=_cparams(("parallel", "parallel")),
        name="inproj",
    )(x_all, mod, w1, qn, wuq, kvn, wukv, qg, kg, rope)


def _lru_tile_index(t, nt, reverse):
    if reverse:
        return jnp.where(t == 0, 0, nt - t)
    return t


def _lru_kernel(reverse, nt, lw, x_ref, xp_ref, xn_ref, par_ref, wg_ref, *rest):
    if reverse:
        o_ref, xe_s, a_s, b_s, p_s, h_s, st_s = rest
    else:
        hb_ref, lg_ref, o_ref, xe_s, a_s, b_s, p_s, h_s, st_s = rest
    t = pl.program_id(1)
    tile = _lru_tile_index(t, nt, reverse)
    nchunk = lw // LANES

    prev_ok = tile >= 2
    next_ok = jnp.logical_and(tile >= 1, tile <= nt - 2)
    xe_s[0:LRU_HALO, :] = jnp.where(prev_ok, xp_ref[...], 0.0)
    xe_s[LRU_HALO:LRU_HALO + TILE, :] = x_ref[...]
    xe_s[LRU_HALO + TILE:, :] = jnp.where(next_ok, xn_ref[...], 0.0)
    ba = par_ref[0:1, :]
    bx = par_ref[1:2, :]
    lam = par_ref[2:3, :]
    u = jnp.zeros((TILE, lw), F32) + par_ref[3:4, :]
    for k in range(LRU_CONV):
        off = LRU_HALO - LRU_CONV // 2 + k
        u = u + xe_s[off:off + TILE, :] * par_ref[4 + k:5 + k, :]

    sp = jnp.maximum(-lam, 0.0) + jnp.log1p(jnp.exp(-jnp.abs(lam)))
    decay = (-LRU_C) * sp

    @pl.when(t == 0)
    def _():
        st_s[...] = jnp.zeros_like(st_s)

    for c in range(nchunk):
        cs = slice(c * LANES, (c + 1) * LANES)
        uc = u[:, cs]
        g = _dot(uc.astype(BF16), wg_ref[c])
        r = _sigmoid(g[:, 0:LANES] + ba[:, cs])
        i = _sigmoid(g[:, LANES:] + bx[:, cs])
        a = jnp.exp(r * decay[:, cs])
        bb = jnp.sqrt(1.0 - a * a) * (i * uc)
        for s in range(SCAN_SEGS):
            rows = slice(s * SEG_LEN, (s + 1) * SEG_LEN)
            a_s[c, s * SEG_PITCH:s * SEG_PITCH + SEG_LEN, :] = a[rows]
            b_s[c, s * SEG_PITCH:s * SEG_PITCH + SEG_LEN, :] = bb[rows]

    def step(jj, carry):
        j = (SEG_LEN - 1 - jj) if reverse else jj
        new = []
        for c in range(nchunk):
            hh, pp = carry[2 * c], carry[2 * c + 1]
            av = a_s[c, pl.ds(j, SCAN_SEGS, stride=SEG_PITCH), :]
            bv = b_s[c, pl.ds(j, SCAN_SEGS, stride=SEG_PITCH), :]
            hh = av * hh + bv
            pp = av * pp
            h_s[c, pl.ds(j, SCAN_SEGS, stride=SEG_PITCH), :] = hh
            p_s[c, pl.ds(j, SCAN_SEGS, stride=SEG_PITCH), :] = pp
            new += [hh, pp]
        return tuple(new)

    init = []
    for c in range(nchunk):
        init += [jnp.zeros((SCAN_SEGS, LANES), F32), jnp.ones((SCAN_SEGS, LANES), F32)]
    fin = lax.fori_loop(0, SEG_LEN, step, tuple(init))

    order = range(SCAN_SEGS - 1, -1, -1) if reverse else range(SCAN_SEGS)
    for c in range(nchunk):
        cs = slice(c * LANES, (c + 1) * LANES)
        hfin, pfin = fin[2 * c], fin[2 * c + 1]
        carry = st_s[:, cs]
        for s in order:
            base = s * SEG_PITCH
            hseg = h_s[c, base:base + SEG_LEN, :] + p_s[c, base:base + SEG_LEN, :] * carry
            rows = slice(s * SEG_LEN, (s + 1) * SEG_LEN)
            if reverse:
                o_ref[rows, cs] = hseg
            else:
                tot = hseg + hb_ref[rows, cs]
                o_ref[rows, cs] = (tot * _gelu_tanh(lg_ref[rows, cs])).astype(o_ref.dtype)
            carry = hfin[s:s + 1, :] + pfin[s:s + 1, :] * carry
        st_s[:, cs] = carry


def _lru_call(reverse, lx, par, wg, hb=None, lg=None):
    b, n, lw = lx.shape
    nt = n // TILE
    nrow8 = n // LRU_HALO
    per = TILE // LRU_HALO
    tidx = lambda t: _lru_tile_index(t, nt, reverse)
    tok = pl.BlockSpec((None, TILE, lw), lambda bb, t: (bb, tidx(t), 0))
    prev = pl.BlockSpec((None, LRU_HALO, lw), lambda bb, t: (bb, jnp.maximum(tidx(t) * per - 1, 0), 0))
    nxt = pl.BlockSpec((None, LRU_HALO, lw),
                       lambda bb, t: (bb, jnp.minimum((tidx(t) + 1) * per, nrow8 - 1), 0))
    nchunk = lw // LANES
    seg_rows = SCAN_SEGS * SEG_PITCH
    scratch = [
        pltpu.VMEM((TILE + 2 * LRU_HALO, lw), F32),
        pltpu.VMEM((nchunk, seg_rows, LANES), F32),
        pltpu.VMEM((nchunk, seg_rows, LANES), F32),
        pltpu.VMEM((nchunk, seg_rows, LANES), F32),
        pltpu.VMEM((nchunk, seg_rows, LANES), F32),
        pltpu.VMEM((1, lw), F32),
    ]
    in_specs = [tok, prev, nxt, _const_spec(par.shape), _const_spec(wg.shape)]
    args = [lx, lx, lx, par, wg]
    if reverse:
        out_dtype = F32
    else:
        in_specs += [tok, tok]
        args += [hb, lg]
        out_dtype = BF16
    return pl.pallas_call(
        functools.partial(_lru_kernel, reverse, nt, lw),
        grid=(b, nt),
        in_specs=in_specs,
        out_specs=tok,
        out_shape=jax.ShapeDtypeStruct((b, n, lw), out_dtype),
        scratch_shapes=scratch,
        compiler_params=_cparams(("parallel", "arbitrary")),
        name="lru_bwd" if reverse else "lru_fwd",
    )(*args)


def _conf_kernel(nt, cw, x_ref, xp_ref, xn_ref, w_ref, par_ref, o_ref, xe_s, sh_s, cv_s):
    t = pl.program_id(1)
    prev_ok = t >= 2
    next_ok = jnp.logical_and(t >= 1, t <= nt - 2)
    xe_s[0:CONV_HALO, :] = jnp.where(prev_ok, xp_ref[...], 0.0)
    xe_s[CONV_HALO:CONV_HALO + TILE, :] = x_ref[...]
    xe_s[CONV_HALO + TILE:, :] = jnp.where(next_ok, xn_ref[...], 0.0)
    ext = TILE + 2 * CONV_HALO - SUBLANES
    for r in range(1, SUBLANES):
        sh_s[r - 1] = xe_s[r:r + ext, :]

    def block(rb, carry):
        row0 = pl.multiple_of(rb * CONF_ROWS, CONF_ROWS)
        acc = jnp.zeros((CONF_ROWS, cw), F32) + par_ref[0:1, :]
        for k in range(CONF_K):
            off = CONV_HALO - CONF_K // 2 + k
            r = off % SUBLANES
            start = row0 + (off - r)
            if r == 0:
                win = xe_s[pl.ds(start, CONF_ROWS), :]
            else:
                win = sh_s[r - 1, pl.ds(start, CONF_ROWS), :]
            acc = acc + win * w_ref[k:k + 1, :]
        cv_s[pl.ds(row0, CONF_ROWS), :] = acc
        return carry

    lax.fori_loop(0, TILE // CONF_ROWS, block, 0)
    cv = cv_s[...]
    mu = jnp.mean(cv, axis=-1, keepdims=True)
    xc = cv - mu
    var = jnp.mean(xc * xc, axis=-1, keepdims=True)
    y = xc * lax.rsqrt(var + EPS) * par_ref[1:2, :] + par_ref[2:3, :]
    o_ref[...] = _silu(y).astype(o_ref.dtype)


def _conf_call(glu, w, par):
    b, n, cw = glu.shape
    nt = n // TILE
    nrow = n // CONV_HALO
    per = TILE // CONV_HALO
    tok = pl.BlockSpec((None, TILE, cw), lambda bb, t: (bb, t, 0))
    prev = pl.BlockSpec((None, CONV_HALO, cw), lambda bb, t: (bb, jnp.maximum(t * per - 1, 0), 0))
    nxt = pl.BlockSpec((None, CONV_HALO, cw), lambda bb, t: (bb, jnp.minimum((t + 1) * per, nrow - 1), 0))
    return pl.pallas_call(
        functools.partial(_conf_kernel, nt, cw),
        grid=(b, nt),
        in_specs=[tok, prev, nxt, _const_spec(w.shape), _const_spec(par.shape)],
        out_specs=tok,
        out_shape=jax.ShapeDtypeStruct((b, n, cw), BF16),
        scratch_shapes=[pltpu.VMEM((TILE + 2 * CONV_HALO, cw), F32),
                        pltpu.VMEM((SUBLANES - 1, TILE + 2 * CONV_HALO - SUBLANES, cw), F32),
                        pltpu.VMEM((TILE, cw), F32)],
        compiler_params=_cparams(("parallel", "parallel")),
        name="conformer",
    )(glu, glu, glu, w, par)


def _attn_kernel(ctx_len, n, tk, unroll, q_ref, k_ref, v_ref, o_ref):
    qt = pl.program_id(2)
    n_lat_chunks = (n - ctx_len) // tk
    nh = q_ref.shape[-1] // HEAD_PAD
    heads = [slice(hd * HEAD_PAD, (hd + 1) * HEAD_PAD) for hd in range(nh)]
    qs = [q_ref[:, hs] for hs in heads]

    def scores(q, kc):
        return lax.dot_general(q, kc, (((1,), (1,)), ((), ())), preferred_element_type=F32)

    def scores_and_max(q, kc):
        s = scores(q, kc)
        return s, jnp.max(s, axis=-1, keepdims=True)

    def update(s, smax, vc, m, acc):
        m_new = jnp.maximum(m, smax)
        alpha = jnp.exp2(m - m_new)
        p = jnp.exp2(s - m_new)
        return m_new, alpha * acc + _dot(p.astype(BF16), vc)

    ctx_sc = [scores_and_max(q, k_ref[0:ctx_len, hs]) for q, hs in zip(qs, heads)]
    first = [scores_and_max(q, k_ref[ctx_len:ctx_len + tk, hs]) for q, hs in zip(qs, heads)]
    ms, accs = [], []
    for (s, smax), hs in zip(ctx_sc, heads):
        m0 = jnp.full((TILE, 1), NEG_BIG, F32)
        a0 = jnp.zeros((TILE, HEAD_PAD), F32)
        m1, a1 = update(s, smax, v_ref[0:ctx_len, hs], m0, a0)
        ms.append(m1)
        accs.append(a1)


    def body(jj, carry):
        ms, accs = list(carry[0]), list(carry[1])
        s_cur, smax_cur = list(carry[2]), list(carry[3])
        for u in range(unroll):
            c = jj * unroll + u
            cur = pl.multiple_of(ctx_len + c * tk, tk)
            nxt = pl.multiple_of(ctx_len + jnp.minimum(c + 1, n_lat_chunks - 1) * tk, tk)
            ahead = [scores_and_max(q, k_ref[pl.ds(nxt, tk), hs]) for q, hs in zip(qs, heads)]
            for i, hs in enumerate(heads):
                ms[i], accs[i] = update(s_cur[i], smax_cur[i], v_ref[pl.ds(cur, tk), hs], ms[i], accs[i])
            s_cur = [a[0] for a in ahead]
            smax_cur = [a[1] for a in ahead]
        return tuple(ms), tuple(accs), tuple(s_cur), tuple(smax_cur)

    trips = jnp.where(qt == 0, 0, n_lat_chunks // unroll)
    init = (tuple(ms), tuple(accs), tuple(f[0] for f in first), tuple(f[1] for f in first))
    _, accs, _, _ = lax.fori_loop(0, trips, body, init)
    outs = [a * (1.0 / a[:, MLA_V:MLA_V + 1]) for a in accs]
    lane = lax.broadcasted_iota(jnp.int32, outs[0].shape, 1)
    for i in range(nh // 2):
        pair = jnp.where(lane < MLA_V, outs[2 * i], pltpu.roll(outs[2 * i + 1], MLA_V, 1))
        o_ref[:, i * HEAD_PAD:(i + 1) * HEAD_PAD] = pair.astype(o_ref.dtype)


def _attn_call(q, k, v, ctx_len):
    b, n, hq = q.shape
    npair = MLA_HEADS // 2
    nq = n // TILE
    tk = 512
    n_lat_chunks = (n - ctx_len) // tk
    unroll = math.gcd(n_lat_chunks, ATTN_UNROLL)
    return pl.pallas_call(
        functools.partial(_attn_kernel, ctx_len, n, tk, unroll),
        grid=(b, npair, nq),
        in_specs=[
            pl.BlockSpec((None, TILE, 2 * HEAD_PAD), lambda bb, hp, t: (bb, t, hp)),
            pl.BlockSpec((None, n, 2 * HEAD_PAD), lambda bb, hp, t: (bb, 0, hp)),
            pl.BlockSpec((None, n, 2 * HEAD_PAD), lambda bb, hp, t: (bb, 0, hp)),
        ],
        out_specs=pl.BlockSpec((None, TILE, 2 * MLA_V), lambda bb, hp, t: (bb, t, hp)),
        out_shape=jax.ShapeDtypeStruct((b, n, MLA_HEADS * MLA_V), BF16),
        compiler_params=_cparams(("parallel", "parallel", "arbitrary")),
        name="attention",
    )(q, k, v)


def _merge_kernel(ctx_row, t0, d, x_ref, mod_ref, ya_ref, yb_ref, yc_ref, wgate_ref, wa_ref, wb_ref, wc_ref,
                  wo_ref, o_ref):
    b = pl.program_id(0)
    t = pl.program_id(1) + t0
    row = jnp.where(t == 0, ctx_row, b)
    mod = mod_ref[pl.ds(row, 1), :]
    sh, sc, g1 = mod[:, 0:d], mod[:, d:2 * d], mod[:, 2 * d:3 * d]
    x = x_ref[...]
    h = _rms(x, d) * (1.0 + sc) + sh
    gates = _sigmoid(_dot(h.astype(BF16), wgate_ref[...]))
    y_a = _dot(ya_ref[...], wa_ref[...])
    y_b = _dot(yb_ref[...], wb_ref[...])
    y_c = _dot(yc_ref[...], wc_ref[...])
    mix = gates[:, 0:d] * y_a + gates[:, d:2 * d] * y_b + gates[:, 2 * d:3 * d] * y_c
    o_ref[...] = x + g1 * _dot(mix.astype(BF16), wo_ref[...])


def _merge_call(x_all, mod, ya, yb, yc, wgate, wa, wb, wc, wo, ctx_row, t0):
    b, n, d = x_all.shape
    nt = n // TILE - t0
    tok = lambda w: pl.BlockSpec((None, TILE, w), lambda bb, t: (bb, t + t0, 0))
    tok_out = pl.BlockSpec((None, TILE, d), lambda bb, t: (bb, t, 0))
    return pl.pallas_call(
        functools.partial(_merge_kernel, ctx_row, t0, d),
        grid=(b, nt),
        in_specs=[tok(d), _const_spec(mod.shape), tok(ya.shape[-1]), tok(yb.shape[-1]), tok(yc.shape[-1]),
                  _const_spec(wgate.shape), _const_spec(wa.shape), _const_spec(wb.shape),
                  _const_spec(wc.shape), _const_spec(wo.shape)],
        out_specs=tok_out,
        out_shape=jax.ShapeDtypeStruct((b, n - t0 * TILE, d), F32),
        compiler_params=_cparams(("parallel", "parallel")),
        name="merge",
    )(x_all, mod, ya, yb, yc, wgate, wa, wb, wc, wo)


def _ffn_kernel(ctx_row, t0, d, x_ref, mod_ref, w1_ref, w2_ref, o_ref):
    b = pl.program_id(0)
    t = pl.program_id(1) + t0
    row = jnp.where(t == 0, ctx_row, b)
    mod = mod_ref[pl.ds(row, 1), :]
    sh, sc, g2 = mod[:, 3 * d:4 * d], mod[:, 4 * d:5 * d], mod[:, 5 * d:6 * d]
    x = x_ref[...]
    h = _rms(x, d) * (1.0 + sc) + sh
    a = jnp.maximum(_dot(h.astype(BF16), w1_ref[...]), 0.0)
    o_ref[...] = x + g2 * _dot((a * a).astype(BF16), w2_ref[...])


def _ffn_call(x_all, mod, w1, w2, ctx_row, t0):
    b, n, d = x_all.shape
    nt = n // TILE
    tok = pl.BlockSpec((None, TILE, d), lambda bb, t: (bb, t, 0))
    return pl.pallas_call(
        functools.partial(_ffn_kernel, ctx_row, t0, d),
        grid=(b, nt),
        in_specs=[tok, _const_spec(mod.shape), _const_spec(w1.shape), _const_spec(w2.shape)],
        out_specs=tok,
        out_shape=jax.ShapeDtypeStruct((b, n, d), F32),
        compiler_params=_cparams(("parallel", "parallel")),
        name="ffn",
    )(x_all, mod, w1, w2)


def _pad_heads(w, width):
    lead = w.shape[:-1]
    w = w.reshape(lead + (MLA_HEADS, width))
    w = jnp.pad(w, [(0, 0)] * len(lead) + [(0, 0), (0, HEAD_PAD - width)])
    return w.reshape(lead + (MLA_HEADS * HEAD_PAD,))


def _block_diag_pairs(w):
    hh, bi, bj = w.shape
    w4 = w.reshape(hh // 2, 2, bi, bj)
    eye = jnp.eye(2, dtype=w.dtype)
    return jnp.einsum('cpij,pq->cpiqj', w4, eye).reshape(hh // 2, 2 * bi, 2 * bj)


def _rope_tables(ctx_len, seq):
    rows = seq // GRID_W
    row = np.repeat(np.arange(rows, dtype=np.float32), GRID_W)
    col = np.tile(np.arange(GRID_W, dtype=np.float32), rows)
    half = MLA_ROPE // 2
    freqs = jnp.asarray(ROPE_THETA, F32) ** (-jnp.arange(0, half, 2, dtype=F32) / half)
    ang = jnp.concatenate([row[:, None] * freqs, col[:, None] * freqs], axis=-1)
    cos, sin = jnp.cos(ang), jnp.sin(ang)
    n = ctx_len + seq
    c = jnp.ones((n, HEAD_PAD), F32)
    c = c.at[ctx_len:, MLA_NOPE:MLA_NOPE + half].set(cos).at[ctx_len:, MLA_NOPE + half:MLA_QK].set(cos)
    s1 = jnp.zeros((n, HEAD_PAD), F32).at[ctx_len:, MLA_NOPE + half:MLA_QK].set(sin)
    s2 = jnp.zeros((n, HEAD_PAD), F32).at[ctx_len:, MLA_NOPE:MLA_NOPE + half].set(-sin)
    return jnp.concatenate([c, s1, s2], axis=-1)


def kernel(x, c, ctx, c_ctx, w_ada, b_ada, w_in, lru_conv_w, lru_conv_b, lru_wa, lru_ba, lru_wx, lru_bx,
           lru_lambda, w_lru_o, mla_q_norm, mla_w_uq, mla_kv_norm, mla_w_ukv, mla_q_gain, mla_k_gain, w_mla_o,
           conf_dw_w, conf_dw_b, conf_ln_g, conf_ln_b, w_conf_o, w_out, w_ff1, w_ff2):
    bsz, seq, d = x.shape
    ctx_len = ctx.shape[1]
    depth = w_ada.shape[0]
    lw = lru_conv_w.shape[-1]
    qr = mla_w_uq.shape[1]
    kvr = mla_w_ukv.shape[1]
    cw = conf_dw_w.shape[-1]
    assert ctx_len == TILE and seq % TILE == 0 and bsz < SUBLANES
    assert lw % LANES == 0 and d % LANES == 0

    ctx_row = bsz
    cc = jnp.zeros((SUBLANES, d), F32).at[:bsz].set(c).at[ctx_row].set(c_ctx)
    mods = _ada_call(cc, w_ada, b_ada)
    rope = _rope_tables(ctx_len, seq)
    x_all = jnp.concatenate([ctx, x], axis=1)
    q_scale = (MLA_QK ** -0.5) * LOG2E

    o_lg, o_cq, o_ckv, o_kr, o_conv, o_gate = (lw, 2 * lw, 2 * lw + qr, 2 * lw + qr + kvr,
                                               2 * lw + qr + kvr + MLA_ROPE,
                                               2 * lw + qr + kvr + MLA_ROPE + 2 * cw)
    for l in range(depth):
        wi = w_in[l]
        kr_pad = jnp.zeros((d, HEAD_PAD), F32).at[:, MLA_NOPE:MLA_QK].set(wi[:, o_kr:o_conv])
        w1 = jnp.concatenate([wi[:, o_cq:o_kr], kr_pad, wi[:, :o_cq], wi[:, o_conv:o_gate]],
                             axis=1).astype(BF16)
        wgate = wi[:, o_gate:].astype(BF16)
        wuq = _pad_heads(mla_w_uq[l], MLA_QK).astype(BF16)
        ukv = mla_w_ukv[l].reshape(kvr, MLA_HEADS, MLA_NOPE + MLA_V)
        wk = _pad_heads(ukv[:, :, :MLA_NOPE].reshape(kvr, MLA_HEADS * MLA_NOPE), MLA_NOPE)
        wv = _pad_heads(ukv[:, :, MLA_NOPE:].reshape(kvr, MLA_HEADS * MLA_V), MLA_V)
        wukv = jnp.concatenate([wk, wv], axis=1).astype(BF16)
        qg = (jnp.pad(mla_q_gain[l], (0, HEAD_PAD - MLA_QK)) * q_scale).reshape(1, HEAD_PAD)
        kg = jnp.pad(mla_k_gain[l], (0, HEAD_PAD - MLA_QK)).reshape(1, HEAD_PAD)
        mod = mods[l]

        lx, lg, glu, q, k, v = _inproj_call(
            x_all, mod, w1, mla_q_norm[l].reshape(1, qr), wuq, mla_kv_norm[l].reshape(1, kvr), wukv,
            qg, kg, rope, ctx_row)

        def lru_par(dd):
            return jnp.concatenate([lru_ba[l, dd][None], lru_bx[l, dd][None], lru_lambda[l, dd][None],
                                    lru_conv_b[l][None], lru_conv_w[l]], axis=0)

        def lru_w(dd):
            return jnp.concatenate([_block_diag_pairs(lru_wa[l, dd]), _block_diag_pairs(lru_wx[l, dd])],
                                   axis=-1).astype(BF16)

        hb = _lru_call(True, lx, lru_par(1), lru_w(1))
        ya = _lru_call(False, lx, lru_par(0), lru_w(0), hb, lg)

        conf_par = jnp.zeros((SUBLANES, cw), F32).at[0].set(conf_dw_b[l]).at[1].set(conf_ln_g[l]).at[2].set(
            conf_ln_b[l])
        conf_w = jnp.zeros((4 * SUBLANES, cw), F32).at[:CONF_K].set(conf_dw_w[l])
        yc = _conf_call(glu, conf_w, conf_par)

        yb = _attn_call(q, k, v, ctx_len)

        t0 = 0 if l < depth - 1 else 1
        x_mid = _merge_call(x_all, mod, ya, yb, yc, wgate, w_lru_o[l].astype(BF16), w_mla_o[l].astype(BF16),
                            w_conf_o[l].astype(BF16), w_out[l].astype(BF16), ctx_row, t0)
        x_all = _ffn_call(x_mid, mod, w_ff1[l].astype(BF16), w_ff2[l].astype(BF16), ctx_row, t0)
    return x_all
```

```python
import functools
import math

import jax
import jax.numpy as jnp
import numpy as np
from jax import lax
from jax.experimental import pallas as pl
from jax.experimental.pallas import tpu as pltpu

F32 = jnp.float32
BF16 = jnp.bfloat16

GRID_W = 64
LRU_HEADS = 8
LRU_CONV = 4
LRU_C = 8.0
MLA_HEADS = 8
MLA_NOPE = 64
MLA_ROPE = 32
MLA_QK = MLA_NOPE + MLA_ROPE
MLA_V = 64
CONF_K = 31
ROPE_THETA = 10000.0
EPS = 1e-6

SUBLANES = 8
LANES = 128
HEAD_PAD = LANES
TILE = 256
SCAN_SEGS = SUBLANES
SEG_LEN = TILE // SCAN_SEGS
SEG_PITCH = SEG_LEN + 4
CONV_HALO = 16
CONF_ROWS = 32
LRU_HALO = 8
ATTN_HEADS_PER_STEP = 2
ATTN_UNROLL = 16
VMEM_LIMIT = 56 * 1024 * 1024
LOG2E = 1.4426950408889634
NEG_BIG = -1e30


def _cparams(sem):
    return pltpu.CompilerParams(dimension_semantics=sem, vmem_limit_bytes=VMEM_LIMIT)


def _layer_spec(arr, idx):
    rest = arr.shape[len(idx):]
    return pl.BlockSpec((None,) * len(idx) + rest, lambda *_: tuple(idx) + (0,) * len(rest))


def _sigmoid(x):
    return 1.0 / (1.0 + jnp.exp(-x))


def _silu(x):
    return x * _sigmoid(x)


def _gelu_tanh(x):
    return 0.5 * x * (1.0 + jnp.tanh(math.sqrt(2.0 / math.pi) * (x + 0.044715 * (x * x * x))))


def _rms(x, width):
    return x * lax.rsqrt(jnp.sum(x * x, axis=-1, keepdims=True) * (1.0 / width) + EPS)


def _dot(a, b):
    return jnp.dot(a, b, preferred_element_type=F32)


def _ada_kernel(c_ref, w_ref, b_ref, o_ref):
    s = _silu(c_ref[...])
    o_ref[...] = _dot(s.astype(BF16), w_ref[...].astype(BF16)) + b_ref[...]


def _ada_call(cc, w_ada, b_ada):
    depth, d, d6 = w_ada.shape
    rows = cc.shape[0]
    nt = d
    return pl.pallas_call(
        _ada_kernel,
        grid=(depth, d6 // nt),
        in_specs=[
            pl.BlockSpec((rows, d), lambda l, j: (0, 0)),
            pl.BlockSpec((None, d, nt), lambda l, j: (l, 0, j)),
            pl.BlockSpec((None, 1, nt), lambda l, j: (l, 0, j)),
        ],
        out_specs=pl.BlockSpec((None, rows, nt), lambda l, j: (l, 0, j)),
        out_shape=jax.ShapeDtypeStruct((depth, rows, d6), F32),
        compiler_params=_cparams(("parallel", "parallel")),
        name="ada",
    )(cc, w_ada, b_ada.reshape(depth, 1, d6))


def _mod_row(mod_ref, ctx_row):
    b = pl.program_id(0)
    t = pl.program_id(1)
    return jnp.where(t == 0, ctx_row, b)


def _rope(xh, c, s1, s2):
    return xh * c + pltpu.roll(xh, 16, 1) * s1 + pltpu.roll(xh, HEAD_PAD - 16, 1) * s2


def _inproj_kernel(ctx_row, dims, x_ref, mod_ref, w1_ref, qn_ref, wuq_ref, kvn_ref, wukv_ref,
                   qg_ref, kg_ref, rope_ref, lx_ref, lg_ref, glu_ref, q_ref, k_ref, v_ref):
    d, lw, qr, kvr, cw = dims
    row = _mod_row(mod_ref, ctx_row)
    mod = mod_ref[pl.ds(row, 1), :]
    sh, sc = mod[:, 0:d], mod[:, d:2 * d]
    h = (_rms(x_ref[...], d) * (1.0 + sc) + sh).astype(BF16)
    n_mla = qr + kvr + HEAD_PAD
    pm = _dot(h, w1_ref[:, 0:n_mla])
    cq = pm[:, 0:qr]
    ckv = pm[:, qr:qr + kvr]
    kr = pm[:, qr + kvr:n_mla]
    cqn = _rms(cq, qr) * qn_ref[...]
    qraw = _dot(cqn.astype(BF16), wuq_ref[...])
    ckvn = _rms(ckv, kvr) * kvn_ref[...]
    kvraw = _dot(ckvn.astype(BF16), wukv_ref[...])

    p = _dot(h, w1_ref[:, n_mla:])
    lx_ref[...] = p[:, 0:lw]
    lg_ref[...] = p[:, lw:2 * lw]
    ca = p[:, 2 * lw:2 * lw + cw]
    cg = p[:, 2 * lw + cw:2 * lw + 2 * cw]
    glu_ref[...] = ca * _sigmoid(cg)

    rc = rope_ref[:, 0:HEAD_PAD]
    rs1 = rope_ref[:, HEAD_PAD:2 * HEAD_PAD]
    rs2 = rope_ref[:, 2 * HEAD_PAD:3 * HEAD_PAD]
    qg = qg_ref[...]
    kg = kg_ref[...]
    one_col = jnp.where(lax.broadcasted_iota(jnp.int32, (1, HEAD_PAD), 1) == MLA_V, 1.0, 0.0)
    for hd in range(MLA_HEADS):
        sl = slice(hd * HEAD_PAD, (hd + 1) * HEAD_PAD)
        qh = _rms(qraw[:, sl], MLA_QK) * qg
        q_ref[:, sl] = _rope(qh, rc, rs1, rs2).astype(BF16)
        kh = _rms(kvraw[:, sl] + kr, MLA_QK) * kg
        k_ref[:, sl] = _rope(kh, rc, rs1, rs2).astype(BF16)
        vo = (MLA_HEADS + hd) * HEAD_PAD
        v_ref[:, sl] = (kvraw[:, vo:vo + HEAD_PAD] + one_col).astype(BF16)


def _inproj_call(layer, x_all, mod, w1, qn, wuq, kvn, wukv, qg, kg, rope, ctx_row):
    b, n, d = x_all.shape
    lw = 512
    qr, kvr, cw = wuq.shape[1], wukv.shape[1], 512
    hq = MLA_HEADS * HEAD_PAD
    par = lambda a: _layer_spec(a, (layer,))
    hv = hq
    nt = n // TILE
    tok = lambda w: pl.BlockSpec((None, TILE, w), lambda bb, t: (bb, t, 0))
    kern = functools.partial(_inproj_kernel, ctx_row, (d, lw, qr, kvr, cw))
    return pl.pallas_call(
        kern,
        grid=(b, nt),
        in_specs=[
            tok(d),
            par(mod), par(w1), par(qn), par(wuq), par(kvn), par(wukv), par(qg), par(kg),
            pl.BlockSpec((TILE, 3 * HEAD_PAD), lambda bb, t: (t, 0)),
        ],
        out_specs=[tok(lw), tok(lw), tok(cw), tok(hq), tok(hq), tok(hv)],
        out_shape=[
            jax.ShapeDtypeStruct((b, n, lw), F32),
            jax.ShapeDtypeStruct((b, n, lw), F32),
            jax.ShapeDtypeStruct((b, n, cw), F32),
            jax.ShapeDtypeStruct((b, n, hq), BF16),
            jax.ShapeDtypeStruct((b, n, hq), BF16),
            jax.ShapeDtypeStruct((b, n, hv), BF16),
        ],
        compiler_params=_cparams(("parallel", "parallel")),
        name="inproj",
    )(x_all, mod, w1, qn, wuq, kvn, wukv, qg, kg, rope)


def _lru_tile_index(t, nt, reverse):
    if reverse:
        return jnp.where(t == 0, 0, nt - t)
    return t


def _lru_kernel(reverse, nt, lw, x_ref, xp_ref, xn_ref, par_ref, wg_ref, *rest):
    if reverse:
        o_ref, xe_s, a_s, b_s, p_s, h_s, st_s = rest
    else:
        hb_ref, lg_ref, o_ref, xe_s, a_s, b_s, p_s, h_s, st_s = rest
    t = pl.program_id(1)
    tile = _lru_tile_index(t, nt, reverse)
    nchunk = lw // LANES

    prev_ok = tile >= 2
    next_ok = jnp.logical_and(tile >= 1, tile <= nt - 2)
    xe_s[0:LRU_HALO, :] = jnp.where(prev_ok, xp_ref[...], 0.0)
    xe_s[LRU_HALO:LRU_HALO + TILE, :] = x_ref[...]
    xe_s[LRU_HALO + TILE:, :] = jnp.where(next_ok, xn_ref[...], 0.0)
    ba = par_ref[0:1, :]
    bx = par_ref[1:2, :]
    lam = par_ref[2:3, :]
    u = jnp.zeros((TILE, lw), F32) + par_ref[3:4, :]
    for k in range(LRU_CONV):
        off = LRU_HALO - LRU_CONV // 2 + k
        u = u + xe_s[off:off + TILE, :] * par_ref[4 + k:5 + k, :]

    sp = jnp.maximum(-lam, 0.0) + jnp.log1p(jnp.exp(-jnp.abs(lam)))
    decay = (-LRU_C) * sp

    @pl.when(t == 0)
    def _():
        st_s[...] = jnp.zeros_like(st_s)

    for c in range(nchunk):
        cs = slice(c * LANES, (c + 1) * LANES)
        uc = u[:, cs]
        g = _dot(uc.astype(BF16), wg_ref[c])
        r = _sigmoid(g[:, 0:LANES] + ba[:, cs])
        i = _sigmoid(g[:, LANES:] + bx[:, cs])
        a = jnp.exp(r * decay[:, cs])
        bb = jnp.sqrt(1.0 - a * a) * (i * uc)
        for s in range(SCAN_SEGS):
            rows = slice(s * SEG_LEN, (s + 1) * SEG_LEN)
            a_s[c, s * SEG_PITCH:s * SEG_PITCH + SEG_LEN, :] = a[rows]
            b_s[c, s * SEG_PITCH:s * SEG_PITCH + SEG_LEN, :] = bb[rows]

    def step(jj, carry):
        j = (SEG_LEN - 1 - jj) if reverse else jj
        new = []
        for c in range(nchunk):
            hh, pp = carry[2 * c], carry[2 * c + 1]
            av = a_s[c, pl.ds(j, SCAN_SEGS, stride=SEG_PITCH), :]
            bv = b_s[c, pl.ds(j, SCAN_SEGS, stride=SEG_PITCH), :]
            hh = av * hh + bv
            pp = av * pp
            h_s[c, pl.ds(j, SCAN_SEGS, stride=SEG_PITCH), :] = hh
            p_s[c, pl.ds(j, SCAN_SEGS, stride=SEG_PITCH), :] = pp
            new += [hh, pp]
        return tuple(new)

    init = []
    for c in range(nchunk):
        init += [jnp.zeros((SCAN_SEGS, LANES), F32), jnp.ones((SCAN_SEGS, LANES), F32)]
    fin = lax.fori_loop(0, SEG_LEN, step, tuple(init))

    order = range(SCAN_SEGS - 1, -1, -1) if reverse else range(SCAN_SEGS)
    for c in range(nchunk):
        cs = slice(c * LANES, (c + 1) * LANES)
        hfin, pfin = fin[2 * c], fin[2 * c + 1]
        carry = st_s[:, cs]
        for s in order:
            base = s * SEG_PITCH
            hseg = h_s[c, base:base + SEG_LEN, :] + p_s[c, base:base + SEG_LEN, :] * carry
            rows = slice(s * SEG_LEN, (s + 1) * SEG_LEN)
            if reverse:
                o_ref[rows, cs] = hseg
            else:
                tot = hseg + hb_ref[rows, cs]
                o_ref[rows, cs] = (tot * _gelu_tanh(lg_ref[rows, cs])).astype(o_ref.dtype)
            carry = hfin[s:s + 1, :] + pfin[s:s + 1, :] * carry
        st_s[:, cs] = carry


def _lru_call(layer, reverse, lx, par, wg, hb=None, lg=None):
    b, n, lw = lx.shape
    pidx = (layer, 1 if reverse else 0)
    nt = n // TILE
    nrow8 = n // LRU_HALO
    per = TILE // LRU_HALO
    tidx = lambda t: _lru_tile_index(t, nt, reverse)
    tok = pl.BlockSpec((None, TILE, lw), lambda bb, t: (bb, tidx(t), 0))
    prev = pl.BlockSpec((None, LRU_HALO, lw), lambda bb, t: (bb, jnp.maximum(tidx(t) * per - 1, 0), 0))
    nxt = pl.BlockSpec((None, LRU_HALO, lw),
                       lambda bb, t: (bb, jnp.minimum((tidx(t) + 1) * per, nrow8 - 1), 0))
    nchunk = lw // LANES
    seg_rows = SCAN_SEGS * SEG_PITCH
    scratch = [
        pltpu.VMEM((TILE + 2 * LRU_HALO, lw), F32),
        pltpu.VMEM((nchunk, seg_rows, LANES), F32),
        pltpu.VMEM((nchunk, seg_rows, LANES), F32),
        pltpu.VMEM((nchunk, seg_rows, LANES), F32),
        pltpu.VMEM((nchunk, seg_rows, LANES), F32),
        pltpu.VMEM((1, lw), F32),
    ]
    in_specs = [tok, prev, nxt, _layer_spec(par, pidx), _layer_spec(wg, pidx)]
    args = [lx, lx, lx, par, wg]
    if reverse:
        out_dtype = F32
    else:
        in_specs += [tok, tok]
        args += [hb, lg]
        out_dtype = BF16
    return pl.pallas_call(
        functools.partial(_lru_kernel, reverse, nt, lw),
        grid=(b, nt),
        in_specs=in_specs,
        out_specs=tok,
        out_shape=jax.ShapeDtypeStruct((b, n, lw), out_dtype),
        scratch_shapes=scratch,
        compiler_params=_cparams(("parallel", "arbitrary")),
        name="lru_bwd" if reverse else "lru_fwd",
    )(*args)


def _conf_kernel(nt, cw, x_ref, xp_ref, xn_ref, w_ref, par_ref, o_ref, xe_s, sh_s, cv_s):
    t = pl.program_id(1)
    prev_ok = t >= 2
    next_ok = jnp.logical_and(t >= 1, t <= nt - 2)
    xe_s[0:CONV_HALO, :] = jnp.where(prev_ok, xp_ref[...], 0.0)
    xe_s[CONV_HALO:CONV_HALO + TILE, :] = x_ref[...]
    xe_s[CONV_HALO + TILE:, :] = jnp.where(next_ok, xn_ref[...], 0.0)
    ext = TILE + 2 * CONV_HALO - SUBLANES
    for r in range(1, SUBLANES):
        sh_s[r - 1] = xe_s[r:r + ext, :]

    def block(rb, carry):
        row0 = pl.multiple_of(rb * CONF_ROWS, CONF_ROWS)
        acc = jnp.zeros((CONF_ROWS, cw), F32) + par_ref[0:1, :]
        for k in range(CONF_K):
            off = CONV_HALO - CONF_K // 2 + k
            r = off % SUBLANES
            start = row0 + (off - r)
            if r == 0:
                win = xe_s[pl.ds(start, CONF_ROWS), :]
            else:
                win = sh_s[r - 1, pl.ds(start, CONF_ROWS), :]
            acc = acc + win * w_ref[k:k + 1, :]
        cv_s[pl.ds(row0, CONF_ROWS), :] = acc
        return carry

    lax.fori_loop(0, TILE // CONF_ROWS, block, 0)
    cv = cv_s[...]
    mu = jnp.mean(cv, axis=-1, keepdims=True)
    xc = cv - mu
    var = jnp.mean(xc * xc, axis=-1, keepdims=True)
    y = xc * lax.rsqrt(var + EPS) * par_ref[1:2, :] + par_ref[2:3, :]
    o_ref[...] = _silu(y).astype(o_ref.dtype)


def _conf_call(layer, glu, w, par):
    b, n, cw = glu.shape
    nt = n // TILE
    nrow = n // CONV_HALO
    per = TILE // CONV_HALO
    tok = pl.BlockSpec((None, TILE, cw), lambda bb, t: (bb, t, 0))
    prev = pl.BlockSpec((None, CONV_HALO, cw), lambda bb, t: (bb, jnp.maximum(t * per - 1, 0), 0))
    nxt = pl.BlockSpec((None, CONV_HALO, cw), lambda bb, t: (bb, jnp.minimum((t + 1) * per, nrow - 1), 0))
    return pl.pallas_call(
        functools.partial(_conf_kernel, nt, cw),
        grid=(b, nt),
        in_specs=[tok, prev, nxt, _layer_spec(w, (layer,)), _layer_spec(par, (layer,))],
        out_specs=tok,
        out_shape=jax.ShapeDtypeStruct((b, n, cw), BF16),
        scratch_shapes=[pltpu.VMEM((TILE + 2 * CONV_HALO, cw), F32),
                        pltpu.VMEM((SUBLANES - 1, TILE + 2 * CONV_HALO - SUBLANES, cw), F32),
                        pltpu.VMEM((TILE, cw), F32)],
        compiler_params=_cparams(("parallel", "parallel")),
        name="conformer",
    )(glu, glu, glu, w, par)


def _attn_kernel(ctx_len, n, tk, unroll, q_ref, k_ref, v_ref, o_ref):
    qt = pl.program_id(2)
    n_lat_chunks = (n - ctx_len) // tk
    nh = q_ref.shape[-1] // HEAD_PAD
    heads = [slice(hd * HEAD_PAD, (hd + 1) * HEAD_PAD) for hd in range(nh)]
    qs = [q_ref[:, hs] for hs in heads]

    def scores(q, kc):
        return lax.dot_general(q, kc, (((1,), (1,)), ((), ())), preferred_element_type=F32)

    def scores_and_max(q, kc):
        s = scores(q, kc)
        return s, jnp.max(s, axis=-1, keepdims=True)

    def update(s, smax, vc, m, acc):
        m_new = jnp.maximum(m, smax)
        alpha = jnp.exp2(m - m_new)
        p = jnp.exp2(s - m_new)
        return m_new, alpha * acc + _dot(p.astype(BF16), vc)

    ctx_sc = [scores_and_max(q, k_ref[0:ctx_len, hs]) for q, hs in zip(qs, heads)]
    first = [scores_and_max(q, k_ref[ctx_len:ctx_len + tk, hs]) for q, hs in zip(qs, heads)]
    ms, accs = [], []
    for (s, smax), hs in zip(ctx_sc, heads):
        m0 = jnp.full((TILE, 1), NEG_BIG, F32)
        a0 = jnp.zeros((TILE, HEAD_PAD), F32)
        m1, a1 = update(s, smax, v_ref[0:ctx_len, hs], m0, a0)
        ms.append(m1)
        accs.append(a1)


    def body(jj, carry):
        ms, accs = list(carry[0]), list(carry[1])
        s_cur, smax_cur = list(carry[2]), list(carry[3])
        for u in range(unroll):
            c = jj * unroll + u
            cur = pl.multiple_of(ctx_len + c * tk, tk)
            nxt = pl.multiple_of(ctx_len + jnp.minimum(c + 1, n_lat_chunks - 1) * tk, tk)
            ahead = [scores_and_max(q, k_ref[pl.ds(nxt, tk), hs]) for q, hs in zip(qs, heads)]
            for i, hs in enumerate(heads):
                ms[i], accs[i] = update(s_cur[i], smax_cur[i], v_ref[pl.ds(cur, tk), hs], ms[i], accs[i])
            s_cur = [a[0] for a in ahead]
            smax_cur = [a[1] for a in ahead]
        return tuple(ms), tuple(accs), tuple(s_cur), tuple(smax_cur)

    trips = jnp.where(qt == 0, 0, n_lat_chunks // unroll)
    init = (tuple(ms), tuple(accs), tuple(f[0] for f in first), tuple(f[1] for f in first))
    _, accs, _, _ = lax.fori_loop(0, trips, body, init)
    outs = [a * (1.0 / a[:, MLA_V:MLA_V + 1]) for a in accs]
    lane = lax.broadcasted_iota(jnp.int32, outs[0].shape, 1)
    for i in range(nh // 2):
        pair = jnp.where(lane < MLA_V, outs[2 * i], pltpu.roll(outs[2 * i + 1], MLA_V, 1))
        o_ref[:, i * HEAD_PAD:(i + 1) * HEAD_PAD] = pair.astype(o_ref.dtype)


def _attn_call(q, k, v, ctx_len):
    b, n, hq = q.shape
    hps = ATTN_HEADS_PER_STEP
    ngroup = MLA_HEADS // hps
    nq = n // TILE
    tk = 512
    n_lat_chunks = (n - ctx_len) // tk
    unroll = math.gcd(n_lat_chunks, ATTN_UNROLL)
    return pl.pallas_call(
        functools.partial(_attn_kernel, ctx_len, n, tk, unroll),
        grid=(b, ngroup, nq),
        in_specs=[
            pl.BlockSpec((None, TILE, hps * HEAD_PAD), lambda bb, hp, t: (bb, t, hp)),
            pl.BlockSpec((None, n, hps * HEAD_PAD), lambda bb, hp, t: (bb, 0, hp)),
            pl.BlockSpec((None, n, hps * HEAD_PAD), lambda bb, hp, t: (bb, 0, hp)),
        ],
        out_specs=pl.BlockSpec((None, TILE, hps * MLA_V), lambda bb, hp, t: (bb, t, hp)),
        out_shape=jax.ShapeDtypeStruct((b, n, MLA_HEADS * MLA_V), BF16),
        compiler_params=_cparams(("parallel", "parallel", "arbitrary")),
        name="attention",
    )(q, k, v)


def _merge_kernel(ctx_row, t0, d, x_ref, mod_ref, ya_ref, yb_ref, yc_ref, wgate_ref, wa_ref, wb_ref, wc_ref,
                  wo_ref, o_ref):
    b = pl.program_id(0)
    t = pl.program_id(1) + t0
    row = jnp.where(t == 0, ctx_row, b)
    mod = mod_ref[pl.ds(row, 1), :]
    sh, sc, g1 = mod[:, 0:d], mod[:, d:2 * d], mod[:, 2 * d:3 * d]
    x = x_ref[...]
    h = _rms(x, d) * (1.0 + sc) + sh
    gates = _sigmoid(_dot(h.astype(BF16), wgate_ref[...]))
    y_a = _dot(ya_ref[...], wa_ref[...])
    y_b = _dot(yb_ref[...], wb_ref[...])
    y_c = _dot(yc_ref[...], wc_ref[...])
    mix = gates[:, 0:d] * y_a + gates[:, d:2 * d] * y_b + gates[:, 2 * d:3 * d] * y_c
    o_ref[...] = x + g1 * _dot(mix.astype(BF16), wo_ref[...])


def _merge_call(layer, x_all, mod, ya, yb, yc, wgate, wa, wb, wc, wo, ctx_row, t0):
    b, n, d = x_all.shape
    nt = n // TILE - t0
    par = lambda a: _layer_spec(a, (layer,))
    tok = lambda w: pl.BlockSpec((None, TILE, w), lambda bb, t: (bb, t + t0, 0))
    tok_out = pl.BlockSpec((None, TILE, d), lambda bb, t: (bb, t, 0))
    return pl.pallas_call(
        functools.partial(_merge_kernel, ctx_row, t0, d),
        grid=(b, nt),
        in_specs=[tok(d), par(mod), tok(ya.shape[-1]), tok(yb.shape[-1]), tok(yc.shape[-1]),
                  par(wgate), par(wa), par(wb), par(wc), par(wo)],
        out_specs=tok_out,
        out_shape=jax.ShapeDtypeStruct((b, n - t0 * TILE, d), F32),
        compiler_params=_cparams(("parallel", "parallel")),
        name="merge",
    )(x_all, mod, ya, yb, yc, wgate, wa, wb, wc, wo)


def _ffn_kernel(ctx_row, t0, d, x_ref, mod_ref, w1_ref, w2_ref, o_ref):
    b = pl.program_id(0)
    t = pl.program_id(1) + t0
    row = jnp.where(t == 0, ctx_row, b)
    mod = mod_ref[pl.ds(row, 1), :]
    sh, sc, g2 = mod[:, 3 * d:4 * d], mod[:, 4 * d:5 * d], mod[:, 5 * d:6 * d]
    x = x_ref[...]
    h = _rms(x, d) * (1.0 + sc) + sh
    a = jnp.maximum(_dot(h.astype(BF16), w1_ref[...]), 0.0)
    o_ref[...] = x + g2 * _dot((a * a).astype(BF16), w2_ref[...])


def _ffn_call(layer, x_all, mod, w1, w2, ctx_row, t0):
    b, n, d = x_all.shape
    nt = n // TILE
    tok = pl.BlockSpec((None, TILE, d), lambda bb, t: (bb, t, 0))
    par = lambda a: _layer_spec(a, (layer,))
    return pl.pallas_call(
        functools.partial(_ffn_kernel, ctx_row, t0, d),
        grid=(b, nt),
        in_specs=[tok, par(mod), par(w1), par(w2)],
        out_specs=tok,
        out_shape=jax.ShapeDtypeStruct((b, n, d), F32),
        compiler_params=_cparams(("parallel", "parallel")),
        name="ffn",
    )(x_all, mod, w1, w2)


def _pad_heads(w, width):
    lead = w.shape[:-1]
    w = w.reshape(lead + (MLA_HEADS, width))
    w = jnp.pad(w, [(0, 0)] * len(lead) + [(0, 0), (0, HEAD_PAD - width)])
    return w.reshape(lead + (MLA_HEADS * HEAD_PAD,))


def _block_diag_pairs(w):
    lead, (hh, bi, bj) = w.shape[:-3], w.shape[-3:]
    w4 = w.reshape(lead + (hh // 2, 2, bi, bj))
    eye = jnp.eye(2, dtype=w.dtype)
    return jnp.einsum('...cpij,pq->...cpiqj', w4, eye).reshape(lead + (hh // 2, 2 * bi, 2 * bj))


def _rope_tables(ctx_len, seq):
    rows = seq // GRID_W
    row = np.repeat(np.arange(rows, dtype=np.float32), GRID_W)
    col = np.tile(np.arange(GRID_W, dtype=np.float32), rows)
    half = MLA_ROPE // 2
    freqs = jnp.asarray(ROPE_THETA, F32) ** (-jnp.arange(0, half, 2, dtype=F32) / half)
    ang = jnp.concatenate([row[:, None] * freqs, col[:, None] * freqs], axis=-1)
    cos, sin = jnp.cos(ang), jnp.sin(ang)
    n = ctx_len + seq
    c = jnp.ones((n, HEAD_PAD), F32)
    c = c.at[ctx_len:, MLA_NOPE:MLA_NOPE + half].set(cos).at[ctx_len:, MLA_NOPE + half:MLA_QK].set(cos)
    s1 = jnp.zeros((n, HEAD_PAD), F32).at[ctx_len:, MLA_NOPE + half:MLA_QK].set(sin)
    s2 = jnp.zeros((n, HEAD_PAD), F32).at[ctx_len:, MLA_NOPE:MLA_NOPE + half].set(-sin)
    return jnp.concatenate([c, s1, s2], axis=-1)


def kernel(x, c, ctx, c_ctx, w_ada, b_ada, w_in, lru_conv_w, lru_conv_b, lru_wa, lru_ba, lru_wx, lru_bx,
           lru_lambda, w_lru_o, mla_q_norm, mla_w_uq, mla_kv_norm, mla_w_ukv, mla_q_gain, mla_k_gain, w_mla_o,
           conf_dw_w, conf_dw_b, conf_ln_g, conf_ln_b, w_conf_o, w_out, w_ff1, w_ff2):
    bsz, seq, d = x.shape
    ctx_len = ctx.shape[1]
    depth = w_ada.shape[0]
    lw = lru_conv_w.shape[-1]
    qr = mla_w_uq.shape[1]
    kvr = mla_w_ukv.shape[1]
    cw = conf_dw_w.shape[-1]
    assert ctx_len == TILE and seq % TILE == 0 and bsz < SUBLANES
    assert lw % LANES == 0 and d % LANES == 0

    ctx_row = bsz
    cc = jnp.zeros((SUBLANES, d), F32).at[:bsz].set(c).at[ctx_row].set(c_ctx)
    mods = _ada_call(cc, w_ada, b_ada)
    rope = _rope_tables(ctx_len, seq)
    x_all = jnp.concatenate([ctx, x], axis=1)
    q_scale = (MLA_QK ** -0.5) * LOG2E

    o_lg, o_cq, o_ckv, o_kr, o_conv, o_gate = (lw, 2 * lw, 2 * lw + qr, 2 * lw + qr + kvr,
                                               2 * lw + qr + kvr + MLA_ROPE,
                                               2 * lw + qr + kvr + MLA_ROPE + 2 * cw)
    kr_pad = jnp.pad(w_in[:, :, o_kr:o_conv], ((0, 0), (0, 0), (MLA_NOPE, HEAD_PAD - MLA_QK)))
    w1 = jnp.concatenate([w_in[:, :, o_cq:o_kr], kr_pad, w_in[:, :, :o_cq], w_in[:, :, o_conv:o_gate]],
                         axis=2).astype(BF16)
    wgate = w_in[:, :, o_gate:].astype(BF16)
    wuq = _pad_heads(mla_w_uq, MLA_QK).astype(BF16)
    ukv = mla_w_ukv.reshape(depth, kvr, MLA_HEADS, MLA_NOPE + MLA_V)
    wk = _pad_heads(ukv[..., :MLA_NOPE].reshape(depth, kvr, MLA_HEADS * MLA_NOPE), MLA_NOPE)
    wv = _pad_heads(ukv[..., MLA_NOPE:].reshape(depth, kvr, MLA_HEADS * MLA_V), MLA_V)
    wukv = jnp.concatenate([wk, wv], axis=2).astype(BF16)
    head_pad = ((0, 0), (0, HEAD_PAD - MLA_QK))
    qg = (jnp.pad(mla_q_gain, head_pad) * q_scale).reshape(depth, 1, HEAD_PAD)
    kg = jnp.pad(mla_k_gain, head_pad).reshape(depth, 1, HEAD_PAD)
    qn = mla_q_norm.reshape(depth, 1, qr)
    kvn = mla_kv_norm.reshape(depth, 1, kvr)
    both = lambda a: jnp.broadcast_to(a[:, None], (depth, 2) + a.shape[1:])
    lru_par = jnp.concatenate([lru_ba[:, :, None], lru_bx[:, :, None], lru_lambda[:, :, None],
                               both(lru_conv_b[:, None]), both(lru_conv_w)], axis=2)
    lru_w = jnp.concatenate([_block_diag_pairs(lru_wa), _block_diag_pairs(lru_wx)], axis=-1).astype(BF16)
    conf_par = jnp.concatenate([conf_dw_b[:, None], conf_ln_g[:, None], conf_ln_b[:, None],
                                jnp.zeros((depth, SUBLANES - 3, cw), F32)], axis=1)
    conf_w = jnp.pad(conf_dw_w, ((0, 0), (0, 4 * SUBLANES - CONF_K), (0, 0)))
    wa_o, wb_o, wc_o = w_lru_o.astype(BF16), w_mla_o.astype(BF16), w_conf_o.astype(BF16)
    wo, wf1, wf2 = w_out.astype(BF16), w_ff1.astype(BF16), w_ff2.astype(BF16)

    for l in range(depth):
        lx, lg, glu, q, k, v = _inproj_call(l, x_all, mods, w1, qn, wuq, kvn, wukv, qg, kg, rope, ctx_row)
        hb = _lru_call(l, True, lx, lru_par, lru_w)
        ya = _lru_call(l, False, lx, lru_par, lru_w, hb, lg)
        yc = _conf_call(l, glu, conf_w, conf_par)
        yb = _attn_call(q, k, v, ctx_len)
        t0 = 0 if l < depth - 1 else 1
        x_mid = _merge_call(l, x_all, mods, ya, yb, yc, wgate, wa_o, wb_o, wc_o, wo, ctx_row, t0)
        x_all = _ffn_call(l, x_mid, mods, wf1, wf2, ctx_row, t0)
    return x_all
```

```python
import functools
import math

import jax
import jax.numpy as jnp
import numpy as np
from jax import lax
from jax.experimental import pallas as pl
from jax.experimental.pallas import tpu as pltpu

F32 = jnp.float32
BF16 = jnp.bfloat16

GRID_W = 64
LRU_HEADS = 8
LRU_CONV = 4
LRU_C = 8.0
MLA_HEADS = 8
MLA_NOPE = 64
MLA_ROPE = 32
MLA_QK = MLA_NOPE + MLA_ROPE
MLA_V = 64
CONF_K = 31
ROPE_THETA = 10000.0
EPS = 1e-6

SUBLANES = 8
LANES = 128
HEAD_PAD = LANES
TILE = 256
SCAN_SEGS = SUBLANES
SEG_LEN = TILE // SCAN_SEGS
SEG_PITCH = SEG_LEN + 4
CONV_HALO = 16
CONF_ROWS = 32
LRU_HALO = 8
ATTN_HEADS_PER_STEP = 2
ATTN_UNROLL = 16
VMEM_LIMIT = 56 * 1024 * 1024
LOG2E = 1.4426950408889634
NEG_BIG = -1e30


def _cparams(sem):
    return pltpu.CompilerParams(dimension_semantics=sem, vmem_limit_bytes=VMEM_LIMIT)


def _layer_spec(arr, idx, mode=None):
    rest = arr.shape[len(idx):]
    kw = {} if mode is None else {"pipeline_mode": mode}
    return pl.BlockSpec((None,) * len(idx) + rest, lambda *_: tuple(idx) + (0,) * len(rest), **kw)


def _sigmoid(x):
    return 1.0 / (1.0 + jnp.exp(-x))


def _silu(x):
    return x * _sigmoid(x)


def _gelu_tanh(x):
    return 0.5 * x * (1.0 + jnp.tanh(math.sqrt(2.0 / math.pi) * (x + 0.044715 * (x * x * x))))


def _rms(x, width):
    return x * lax.rsqrt(jnp.sum(x * x, axis=-1, keepdims=True) * (1.0 / width) + EPS)


def _dot(a, b):
    return jnp.dot(a, b, preferred_element_type=F32)


def _ada_kernel(c_ref, w_ref, b_ref, o_ref):
    s = _silu(c_ref[...])
    o_ref[...] = _dot(s.astype(BF16), w_ref[...].astype(BF16)) + b_ref[...]


def _ada_call(cc, w_ada, b_ada):
    depth, d, d6 = w_ada.shape
    rows = cc.shape[0]
    nt = d
    return pl.pallas_call(
        _ada_kernel,
        grid=(depth, d6 // nt),
        in_specs=[
            pl.BlockSpec((rows, d), lambda l, j: (0, 0)),
            pl.BlockSpec((None, d, nt), lambda l, j: (l, 0, j)),
            pl.BlockSpec((None, 1, nt), lambda l, j: (l, 0, j)),
        ],
        out_specs=pl.BlockSpec((None, rows, nt), lambda l, j: (l, 0, j)),
        out_shape=jax.ShapeDtypeStruct((depth, rows, d6), F32),
        compiler_params=_cparams(("parallel", "parallel")),
        name="ada",
    )(cc, w_ada, b_ada.reshape(depth, 1, d6))


def _mod_row(mod_ref, ctx_row):
    b = pl.program_id(0)
    t = pl.program_id(1)
    return jnp.where(t == 0, ctx_row, b)


def _rope(xh, c, s1, s2):
    return xh * c + pltpu.roll(xh, 16, 1) * s1 + pltpu.roll(xh, HEAD_PAD - 16, 1) * s2


def _inproj_kernel(ctx_row, dims, x_ref, mod_ref, w1_ref, qn_ref, wuq_ref, kvn_ref, wukv_ref,
                   qg_ref, kg_ref, rope_ref, lx_ref, lg_ref, glu_ref, q_ref, k_ref, v_ref):
    d, lw, qr, kvr, cw = dims
    row = _mod_row(mod_ref, ctx_row)
    mod = mod_ref[pl.ds(row, 1), :]
    sh, sc = mod[:, 0:d], mod[:, d:2 * d]
    h = (_rms(x_ref[...], d) * (1.0 + sc) + sh).astype(BF16)
    n_mla = qr + kvr + HEAD_PAD
    pm = _dot(h, w1_ref[:, 0:n_mla])
    cq = pm[:, 0:qr]
    ckv = pm[:, qr:qr + kvr]
    kr = pm[:, qr + kvr:n_mla]
    cqn = _rms(cq, qr) * qn_ref[...]
    qraw = _dot(cqn.astype(BF16), wuq_ref[...])
    ckvn = _rms(ckv, kvr) * kvn_ref[...]
    kvraw = _dot(ckvn.astype(BF16), wukv_ref[...])

    p = _dot(h, w1_ref[:, n_mla:])
    lx_ref[...] = p[:, 0:lw]
    lg_ref[...] = p[:, lw:2 * lw]
    ca = p[:, 2 * lw:2 * lw + cw]
    cg = p[:, 2 * lw + cw:2 * lw + 2 * cw]
    glu_ref[...] = ca * _sigmoid(cg)

    rc = rope_ref[:, 0:HEAD_PAD]
    rs1 = rope_ref[:, HEAD_PAD:2 * HEAD_PAD]
    rs2 = rope_ref[:, 2 * HEAD_PAD:3 * HEAD_PAD]
    qg = qg_ref[...]
    kg = kg_ref[...]
    one_col = jnp.where(lax.broadcasted_iota(jnp.int32, (1, HEAD_PAD), 1) == MLA_V, 1.0, 0.0)
    for hd in range(MLA_HEADS):
        sl = slice(hd * HEAD_PAD, (hd + 1) * HEAD_PAD)
        qh = _rms(qraw[:, sl], MLA_QK) * qg
        q_ref[:, sl] = _rope(qh, rc, rs1, rs2).astype(BF16)
        kh = _rms(kvraw[:, sl] + kr, MLA_QK) * kg
        k_ref[:, sl] = _rope(kh, rc, rs1, rs2).astype(BF16)
        vo = (MLA_HEADS + hd) * HEAD_PAD
        v_ref[:, sl] = (kvraw[:, vo:vo + HEAD_PAD] + one_col).astype(BF16)


def _inproj_call(layer, x_all, mod, w1, qn, wuq, kvn, wukv, qg, kg, rope, ctx_row):
    b, n, d = x_all.shape
    lw = 512
    qr, kvr, cw = wuq.shape[1], wukv.shape[1], 512
    hq = MLA_HEADS * HEAD_PAD
    par = lambda a: _layer_spec(a, (layer,))
    hv = hq
    nt = n // TILE
    tok = lambda w: pl.BlockSpec((None, TILE, w), lambda bb, t: (bb, t, 0))
    kern = functools.partial(_inproj_kernel, ctx_row, (d, lw, qr, kvr, cw))
    return pl.pallas_call(
        kern,
        grid=(b, nt),
        in_specs=[
            tok(d),
            par(mod), par(w1), par(qn), par(wuq), par(kvn), par(wukv), par(qg), par(kg),
            pl.BlockSpec((TILE, 3 * HEAD_PAD), lambda bb, t: (t, 0)),
        ],
        out_specs=[tok(lw), tok(lw), tok(cw), tok(hq), tok(hq), tok(hv)],
        out_shape=[
            jax.ShapeDtypeStruct((b, n, lw), F32),
            jax.ShapeDtypeStruct((b, n, lw), F32),
            jax.ShapeDtypeStruct((b, n, cw), F32),
            jax.ShapeDtypeStruct((b, n, hq), BF16),
            jax.ShapeDtypeStruct((b, n, hq), BF16),
            jax.ShapeDtypeStruct((b, n, hv), BF16),
        ],
        compiler_params=_cparams(("parallel", "parallel")),
        name="inproj",
    )(x_all, mod, w1, qn, wuq, kvn, wukv, qg, kg, rope)


def _lru_tile_index(t, nt, reverse):
    if reverse:
        return jnp.where(t == 0, 0, nt - t)
    return t


def _lru_kernel(reverse, nt, lw, x_ref, xp_ref, xn_ref, par_ref, wg_ref, *rest):
    if reverse:
        o_ref, xe_s, a_s, b_s, p_s, h_s, st_s = rest
    else:
        hb_ref, lg_ref, o_ref, xe_s, a_s, b_s, p_s, h_s, st_s = rest
    t = pl.program_id(1)
    tile = _lru_tile_index(t, nt, reverse)
    nchunk = lw // LANES

    prev_ok = tile >= 2
    next_ok = jnp.logical_and(tile >= 1, tile <= nt - 2)
    xe_s[0:LRU_HALO, :] = jnp.where(prev_ok, xp_ref[...], 0.0)
    xe_s[LRU_HALO:LRU_HALO + TILE, :] = x_ref[...]
    xe_s[LRU_HALO + TILE:, :] = jnp.where(next_ok, xn_ref[...], 0.0)
    ba = par_ref[0:1, :]
    bx = par_ref[1:2, :]
    lam = par_ref[2:3, :]
    u = jnp.zeros((TILE, lw), F32) + par_ref[3:4, :]
    for k in range(LRU_CONV):
        off = LRU_HALO - LRU_CONV // 2 + k
        u = u + xe_s[off:off + TILE, :] * par_ref[4 + k:5 + k, :]

    sp = jnp.maximum(-lam, 0.0) + jnp.log1p(jnp.exp(-jnp.abs(lam)))
    decay = (-LRU_C) * sp

    @pl.when(t == 0)
    def _():
        st_s[...] = jnp.zeros_like(st_s)

    for c in range(nchunk):
        cs = slice(c * LANES, (c + 1) * LANES)
        uc = u[:, cs]
        g = _dot(uc.astype(BF16), wg_ref[c])
        r = _sigmoid(g[:, 0:LANES] + ba[:, cs])
        i = _sigmoid(g[:, LANES:] + bx[:, cs])
        a = jnp.exp(r * decay[:, cs])
        bb = jnp.sqrt(1.0 - a * a) * (i * uc)
        for s in range(SCAN_SEGS):
            rows = slice(s * SEG_LEN, (s + 1) * SEG_LEN)
            a_s[c, s * SEG_PITCH:s * SEG_PITCH + SEG_LEN, :] = a[rows]
            b_s[c, s * SEG_PITCH:s * SEG_PITCH + SEG_LEN, :] = bb[rows]

    def step(jj, carry):
        j = (SEG_LEN - 1 - jj) if reverse else jj
        new = []
        for c in range(nchunk):
            hh, pp = carry[2 * c], carry[2 * c + 1]
            av = a_s[c, pl.ds(j, SCAN_SEGS, stride=SEG_PITCH), :]
            bv = b_s[c, pl.ds(j, SCAN_SEGS, stride=SEG_PITCH), :]
            hh = av * hh + bv
            pp = av * pp
            h_s[c, pl.ds(j, SCAN_SEGS, stride=SEG_PITCH), :] = hh
            p_s[c, pl.ds(j, SCAN_SEGS, stride=SEG_PITCH), :] = pp
            new += [hh, pp]
        return tuple(new)

    init = []
    for c in range(nchunk):
        init += [jnp.zeros((SCAN_SEGS, LANES), F32), jnp.ones((SCAN_SEGS, LANES), F32)]
    fin = lax.fori_loop(0, SEG_LEN, step, tuple(init))

    order = range(SCAN_SEGS - 1, -1, -1) if reverse else range(SCAN_SEGS)
    for c in range(nchunk):
        cs = slice(c * LANES, (c + 1) * LANES)
        hfin, pfin = fin[2 * c], fin[2 * c + 1]
        carry = st_s[:, cs]
        for s in order:
            base = s * SEG_PITCH
            hseg = h_s[c, base:base + SEG_LEN, :] + p_s[c, base:base + SEG_LEN, :] * carry
            rows = slice(s * SEG_LEN, (s + 1) * SEG_LEN)
            if reverse:
                o_ref[rows, cs] = hseg
            else:
                tot = hseg + hb_ref[rows, cs]
                o_ref[rows, cs] = (tot * _gelu_tanh(lg_ref[rows, cs])).astype(o_ref.dtype)
            carry = hfin[s:s + 1, :] + pfin[s:s + 1, :] * carry
        st_s[:, cs] = carry


def _lru_call(layer, reverse, lx, par, wg, hb=None, lg=None):
    b, n, lw = lx.shape
    pidx = (layer, 1 if reverse else 0)
    nt = n // TILE
    nrow8 = n // LRU_HALO
    per = TILE // LRU_HALO
    tidx = lambda t: _lru_tile_index(t, nt, reverse)
    tok = pl.BlockSpec((None, TILE, lw), lambda bb, t: (bb, tidx(t), 0))
    prev = pl.BlockSpec((None, LRU_HALO, lw), lambda bb, t: (bb, jnp.maximum(tidx(t) * per - 1, 0), 0))
    nxt = pl.BlockSpec((None, LRU_HALO, lw),
                       lambda bb, t: (bb, jnp.minimum((tidx(t) + 1) * per, nrow8 - 1), 0))
    nchunk = lw // LANES
    seg_rows = SCAN_SEGS * SEG_PITCH
    scratch = [
        pltpu.VMEM((TILE + 2 * LRU_HALO, lw), F32),
        pltpu.VMEM((nchunk, seg_rows, LANES), F32),
        pltpu.VMEM((nchunk, seg_rows, LANES), F32),
        pltpu.VMEM((nchunk, seg_rows, LANES), F32),
        pltpu.VMEM((nchunk, seg_rows, LANES), F32),
        pltpu.VMEM((1, lw), F32),
    ]
    in_specs = [tok, prev, nxt, _layer_spec(par, pidx), _layer_spec(wg, pidx)]
    args = [lx, lx, lx, par, wg]
    if reverse:
        out_dtype = F32
    else:
        in_specs += [tok, tok]
        args += [hb, lg]
        out_dtype = BF16
    return pl.pallas_call(
        functools.partial(_lru_kernel, reverse, nt, lw),
        grid=(b, nt),
        in_specs=in_specs,
        out_specs=tok,
        out_shape=jax.ShapeDtypeStruct((b, n, lw), out_dtype),
        scratch_shapes=scratch,
        compiler_params=_cparams(("parallel", "arbitrary")),
        name="lru_bwd" if reverse else "lru_fwd",
    )(*args)


def _conf_kernel(nt, cw, x_ref, xp_ref, xn_ref, w_ref, par_ref, o_ref, xe_s, sh_s, cv_s):
    t = pl.program_id(1)
    prev_ok = t >= 2
    next_ok = jnp.logical_and(t >= 1, t <= nt - 2)
    xe_s[0:CONV_HALO, :] = jnp.where(prev_ok, xp_ref[...], 0.0)
    xe_s[CONV_HALO:CONV_HALO + TILE, :] = x_ref[...]
    xe_s[CONV_HALO + TILE:, :] = jnp.where(next_ok, xn_ref[...], 0.0)
    ext = TILE + 2 * CONV_HALO - SUBLANES
    for r in range(1, SUBLANES):
        sh_s[r - 1] = xe_s[r:r + ext, :]

    def block(rb, carry):
        row0 = pl.multiple_of(rb * CONF_ROWS, CONF_ROWS)
        acc = jnp.zeros((CONF_ROWS, cw), F32) + par_ref[0:1, :]
        for k in range(CONF_K):
            off = CONV_HALO - CONF_K // 2 + k
            r = off % SUBLANES
            start = row0 + (off - r)
            if r == 0:
                win = xe_s[pl.ds(start, CONF_ROWS), :]
            else:
                win = sh_s[r - 1, pl.ds(start, CONF_ROWS), :]
            acc = acc + win * w_ref[k:k + 1, :]
        cv_s[pl.ds(row0, CONF_ROWS), :] = acc
        return carry

    lax.fori_loop(0, TILE // CONF_ROWS, block, 0)
    cv = cv_s[...]
    mu = jnp.mean(cv, axis=-1, keepdims=True)
    xc = cv - mu
    var = jnp.mean(xc * xc, axis=-1, keepdims=True)
    y = xc * lax.rsqrt(var + EPS) * par_ref[1:2, :] + par_ref[2:3, :]
    o_ref[...] = _silu(y).astype(o_ref.dtype)


def _conf_call(layer, glu, w, par):
    b, n, cw = glu.shape
    nt = n // TILE
    nrow = n // CONV_HALO
    per = TILE // CONV_HALO
    tok = pl.BlockSpec((None, TILE, cw), lambda bb, t: (bb, t, 0))
    prev = pl.BlockSpec((None, CONV_HALO, cw), lambda bb, t: (bb, jnp.maximum(t * per - 1, 0), 0))
    nxt = pl.BlockSpec((None, CONV_HALO, cw), lambda bb, t: (bb, jnp.minimum((t + 1) * per, nrow - 1), 0))
    return pl.pallas_call(
        functools.partial(_conf_kernel, nt, cw),
        grid=(b, nt),
        in_specs=[tok, prev, nxt, _layer_spec(w, (layer,)), _layer_spec(par, (layer,))],
        out_specs=tok,
        out_shape=jax.ShapeDtypeStruct((b, n, cw), BF16),
        scratch_shapes=[pltpu.VMEM((TILE + 2 * CONV_HALO, cw), F32),
                        pltpu.VMEM((SUBLANES - 1, TILE + 2 * CONV_HALO - SUBLANES, cw), F32),
                        pltpu.VMEM((TILE, cw), F32)],
        compiler_params=_cparams(("parallel", "parallel")),
        name="conformer",
    )(glu, glu, glu, w, par)


def _attn_kernel(ctx_len, n, tk, unroll, q_ref, k_ref, v_ref, o_ref):
    qt = pl.program_id(2)
    n_lat_chunks = (n - ctx_len) // tk
    nh = q_ref.shape[-1] // HEAD_PAD
    heads = [slice(hd * HEAD_PAD, (hd + 1) * HEAD_PAD) for hd in range(nh)]
    qs = [q_ref[:, hs] for hs in heads]

    def scores(q, kc):
        return lax.dot_general(q, kc, (((1,), (1,)), ((), ())), preferred_element_type=F32)

    def scores_and_max(q, kc):
        s = scores(q, kc)
        return s, jnp.max(s, axis=-1, keepdims=True)

    def update(s, smax, vc, m, acc):
        m_new = jnp.maximum(m, smax)
        alpha = jnp.exp2(m - m_new)
        p = jnp.exp2(s - m_new)
        return m_new, alpha * acc + _dot(p.astype(BF16), vc)

    ctx_sc = [scores_and_max(q, k_ref[0:ctx_len, hs]) for q, hs in zip(qs, heads)]
    first = [scores_and_max(q, k_ref[ctx_len:ctx_len + tk, hs]) for q, hs in zip(qs, heads)]
    ms, accs = [], []
    for (s, smax), hs in zip(ctx_sc, heads):
        m0 = jnp.full((TILE, 1), NEG_BIG, F32)
        a0 = jnp.zeros((TILE, HEAD_PAD), F32)
        m1, a1 = update(s, smax, v_ref[0:ctx_len, hs], m0, a0)
        ms.append(m1)
        accs.append(a1)


    def body(jj, carry):
        ms, accs = list(carry[0]), list(carry[1])
        s_cur, smax_cur = list(carry[2]), list(carry[3])
        for u in range(unroll):
            c = jj * unroll + u
            cur = pl.multiple_of(ctx_len + c * tk, tk)
            nxt = pl.multiple_of(ctx_len + jnp.minimum(c + 1, n_lat_chunks - 1) * tk, tk)
            ahead = [scores_and_max(q, k_ref[pl.ds(nxt, tk), hs]) for q, hs in zip(qs, heads)]
            for i, hs in enumerate(heads):
                ms[i], accs[i] = update(s_cur[i], smax_cur[i], v_ref[pl.ds(cur, tk), hs], ms[i], accs[i])
            s_cur = [a[0] for a in ahead]
            smax_cur = [a[1] for a in ahead]
        return tuple(ms), tuple(accs), tuple(s_cur), tuple(smax_cur)

    trips = jnp.where(qt == 0, 0, n_lat_chunks // unroll)
    init = (tuple(ms), tuple(accs), tuple(f[0] for f in first), tuple(f[1] for f in first))
    _, accs, _, _ = lax.fori_loop(0, trips, body, init)
    outs = [a * (1.0 / a[:, MLA_V:MLA_V + 1]) for a in accs]
    lane = lax.broadcasted_iota(jnp.int32, outs[0].shape, 1)
    for i in range(nh // 2):
        pair = jnp.where(lane < MLA_V, outs[2 * i], pltpu.roll(outs[2 * i + 1], MLA_V, 1))
        o_ref[:, i * HEAD_PAD:(i + 1) * HEAD_PAD] = pair.astype(o_ref.dtype)


def _attn_call(q, k, v, ctx_len):
    b, n, hq = q.shape
    hps = ATTN_HEADS_PER_STEP
    ngroup = MLA_HEADS // hps
    nq = n // TILE
    tk = 512
    n_lat_chunks = (n - ctx_len) // tk
    unroll = math.gcd(n_lat_chunks, ATTN_UNROLL)
    return pl.pallas_call(
        functools.partial(_attn_kernel, ctx_len, n, tk, unroll),
        grid=(b, ngroup, nq),
        in_specs=[
            pl.BlockSpec((None, TILE, hps * HEAD_PAD), lambda bb, hp, t: (bb, t, hp)),
            pl.BlockSpec((None, n, hps * HEAD_PAD), lambda bb, hp, t: (bb, 0, hp)),
            pl.BlockSpec((None, n, hps * HEAD_PAD), lambda bb, hp, t: (bb, 0, hp)),
        ],
        out_specs=pl.BlockSpec((None, TILE, hps * MLA_V), lambda bb, hp, t: (bb, t, hp)),
        out_shape=jax.ShapeDtypeStruct((b, n, MLA_HEADS * MLA_V), BF16),
        compiler_params=_cparams(("parallel", "parallel", "arbitrary")),
        name="attention",
    )(q, k, v)


def _merge_kernel(ctx_row, t0, d, x_ref, mod_ref, ya_ref, yb_ref, yc_ref, wgate_ref, wa_ref, wb_ref, wc_ref,
                  wo_ref, w1_ref, w2_ref, o_ref):
    b = pl.program_id(0)
    t = pl.program_id(1) + t0
    row = jnp.where(t == 0, ctx_row, b)
    mod = mod_ref[pl.ds(row, 1), :]
    sh, sc, g1 = mod[:, 0:d], mod[:, d:2 * d], mod[:, 2 * d:3 * d]
    sh2, sc2, g2 = mod[:, 3 * d:4 * d], mod[:, 4 * d:5 * d], mod[:, 5 * d:6 * d]
    x = x_ref[...]
    h = _rms(x, d) * (1.0 + sc) + sh
    gates = _sigmoid(_dot(h.astype(BF16), wgate_ref[...]))
    y_a = _dot(ya_ref[...], wa_ref[...])
    y_b = _dot(yb_ref[...], wb_ref[...])
    y_c = _dot(yc_ref[...], wc_ref[...])
    mix = gates[:, 0:d] * y_a + gates[:, d:2 * d] * y_b + gates[:, 2 * d:3 * d] * y_c
    x1 = x + g1 * _dot(mix.astype(BF16), wo_ref[...])
    h2 = _rms(x1, d) * (1.0 + sc2) + sh2
    a = jnp.maximum(_dot(h2.astype(BF16), w1_ref[...]), 0.0)
    o_ref[...] = x1 + g2 * _dot((a * a).astype(BF16), w2_ref[...])


def _merge_call(layer, x_all, mod, ya, yb, yc, wgate, wa, wb, wc, wo, w1, w2, ctx_row, t0):
    b, n, d = x_all.shape
    nt = n // TILE - t0
    par = lambda a: _layer_spec(a, (layer,))
    wpar = lambda a: _layer_spec(a, (layer,), pl.Buffered(1))
    tok = lambda w: pl.BlockSpec((None, TILE, w), lambda bb, t: (bb, t + t0, 0))
    tok_out = pl.BlockSpec((None, TILE, d), lambda bb, t: (bb, t, 0))
    return pl.pallas_call(
        functools.partial(_merge_kernel, ctx_row, t0, d),
        grid=(b, nt),
        in_specs=[tok(d), par(mod), tok(ya.shape[-1]), tok(yb.shape[-1]), tok(yc.shape[-1]),
                  wpar(wgate), wpar(wa), wpar(wb), wpar(wc), wpar(wo), wpar(w1), wpar(w2)],
        out_specs=tok_out,
        out_shape=jax.ShapeDtypeStruct((b, n - t0 * TILE, d), F32),
        compiler_params=_cparams(("parallel", "parallel")),
        name="merge",
    )(x_all, mod, ya, yb, yc, wgate, wa, wb, wc, wo, w1, w2)


def _pad_heads(w, width):
    lead = w.shape[:-1]
    w = w.reshape(lead + (MLA_HEADS, width))
    w = jnp.pad(w, [(0, 0)] * len(lead) + [(0, 0), (0, HEAD_PAD - width)])
    return w.reshape(lead + (MLA_HEADS * HEAD_PAD,))


def _block_diag_pairs(w):
    lead, (hh, bi, bj) = w.shape[:-3], w.shape[-3:]
    w4 = w.reshape(lead + (hh // 2, 2, bi, bj))
    eye = jnp.eye(2, dtype=w.dtype)
    return jnp.einsum('...cpij,pq->...cpiqj', w4, eye).reshape(lead + (hh // 2, 2 * bi, 2 * bj))


def _rope_tables(ctx_len, seq):
    rows = seq // GRID_W
    row = np.repeat(np.arange(rows, dtype=np.float32), GRID_W)
    col = np.tile(np.arange(GRID_W, dtype=np.float32), rows)
    half = MLA_ROPE // 2
    freqs = jnp.asarray(ROPE_THETA, F32) ** (-jnp.arange(0, half, 2, dtype=F32) / half)
    ang = jnp.concatenate([row[:, None] * freqs, col[:, None] * freqs], axis=-1)
    cos, sin = jnp.cos(ang), jnp.sin(ang)
    n = ctx_len + seq
    c = jnp.ones((n, HEAD_PAD), F32)
    c = c.at[ctx_len:, MLA_NOPE:MLA_NOPE + half].set(cos).at[ctx_len:, MLA_NOPE + half:MLA_QK].set(cos)
    s1 = jnp.zeros((n, HEAD_PAD), F32).at[ctx_len:, MLA_NOPE + half:MLA_QK].set(sin)
    s2 = jnp.zeros((n, HEAD_PAD), F32).at[ctx_len:, MLA_NOPE:MLA_NOPE + half].set(-sin)
    return jnp.concatenate([c, s1, s2], axis=-1)


def kernel(x, c, ctx, c_ctx, w_ada, b_ada, w_in, lru_conv_w, lru_conv_b, lru_wa, lru_ba, lru_wx, lru_bx,
           lru_lambda, w_lru_o, mla_q_norm, mla_w_uq, mla_kv_norm, mla_w_ukv, mla_q_gain, mla_k_gain, w_mla_o,
           conf_dw_w, conf_dw_b, conf_ln_g, conf_ln_b, w_conf_o, w_out, w_ff1, w_ff2):
    bsz, seq, d = x.shape
    ctx_len = ctx.shape[1]
    depth = w_ada.shape[0]
    lw = lru_conv_w.shape[-1]
    qr = mla_w_uq.shape[1]
    kvr = mla_w_ukv.shape[1]
    cw = conf_dw_w.shape[-1]
    assert ctx_len == TILE and seq % TILE == 0 and bsz < SUBLANES
    assert lw % LANES == 0 and d % LANES == 0

    ctx_row = bsz
    cc = jnp.zeros((SUBLANES, d), F32).at[:bsz].set(c).at[ctx_row].set(c_ctx)
    mods = _ada_call(cc, w_ada, b_ada)
    rope = _rope_tables(ctx_len, seq)
    x_all = jnp.concatenate([ctx, x], axis=1)
    q_scale = (MLA_QK ** -0.5) * LOG2E

    o_lg, o_cq, o_ckv, o_kr, o_conv, o_gate = (lw, 2 * lw, 2 * lw + qr, 2 * lw + qr + kvr,
                                               2 * lw + qr + kvr + MLA_ROPE,
                                               2 * lw + qr + kvr + MLA_ROPE + 2 * cw)
    kr_pad = jnp.pad(w_in[:, :, o_kr:o_conv], ((0, 0), (0, 0), (MLA_NOPE, HEAD_PAD - MLA_QK)))
    w1 = jnp.concatenate([w_in[:, :, o_cq:o_kr], kr_pad, w_in[:, :, :o_cq], w_in[:, :, o_conv:o_gate]],
                         axis=2).astype(BF16)
    wgate = w_in[:, :, o_gate:].astype(BF16)
    wuq = _pad_heads(mla_w_uq, MLA_QK).astype(BF16)
    ukv = mla_w_ukv.reshape(depth, kvr, MLA_HEADS, MLA_NOPE + MLA_V)
    wk = _pad_heads(ukv[..., :MLA_NOPE].reshape(depth, kvr, MLA_HEADS * MLA_NOPE), MLA_NOPE)
    wv = _pad_heads(ukv[..., MLA_NOPE:].reshape(depth, kvr, MLA_HEADS * MLA_V), MLA_V)
    wukv = jnp.concatenate([wk, wv], axis=2).astype(BF16)
    head_pad = ((0, 0), (0, HEAD_PAD - MLA_QK))
    qg = (jnp.pad(mla_q_gain, head_pad) * q_scale).reshape(depth, 1, HEAD_PAD)
    kg = jnp.pad(mla_k_gain, head_pad).reshape(depth, 1, HEAD_PAD)
    qn = mla_q_norm.reshape(depth, 1, qr)
    kvn = mla_kv_norm.reshape(depth, 1, kvr)
    both = lambda a: jnp.broadcast_to(a[:, None], (depth, 2) + a.shape[1:])
    lru_par = jnp.concatenate([lru_ba[:, :, None], lru_bx[:, :, None], lru_lambda[:, :, None],
                               both(lru_conv_b[:, None]), both(lru_conv_w)], axis=2)
    lru_w = jnp.concatenate([_block_diag_pairs(lru_wa), _block_diag_pairs(lru_wx)], axis=-1).astype(BF16)
    conf_par = jnp.concatenate([conf_dw_b[:, None], conf_ln_g[:, None], conf_ln_b[:, None],
                                jnp.zeros((depth, SUBLANES - 3, cw), F32)], axis=1)
    conf_w = jnp.pad(conf_dw_w, ((0, 0), (0, 4 * SUBLANES - CONF_K), (0, 0)))
    wa_o, wb_o, wc_o = w_lru_o.astype(BF16), w_mla_o.astype(BF16), w_conf_o.astype(BF16)
    wo, wf1, wf2 = w_out.astype(BF16), w_ff1.astype(BF16), w_ff2.astype(BF16)

    for l in range(depth):
        lx, lg, glu, q, k, v = _inproj_call(l, x_all, mods, w1, qn, wuq, kvn, wukv, qg, kg, rope, ctx_row)
        hb = _lru_call(l, True, lx, lru_par, lru_w)
        ya = _lru_call(l, False, lx, lru_par, lru_w, hb, lg)
        yc = _conf_call(l, glu, conf_w, conf_par)
        yb = _attn_call(q, k, v, ctx_len)
        t0 = 0 if l < depth - 1 else 1
        x_all = _merge_call(l, x_all, mods, ya, yb, yc, wgate, wa_o, wb_o, wc_o, wo, wf1, wf2, ctx_row, t0)
    return x_all
```
